```python
import jax, jax.numpy as jnp
from jax import lax
import numpy as np

D_MODEL = 2048
BATCH = 4
SEQ = 2048
DEPTH = 4
DEC_BATCH = 8
DEC_SEQ = 1
PAST_LEN = 16384
PAGE_SIZE = 128

HEAD_DIM = 128
MIX_WIDTH = D_MODEL
A_WIDTH = MIX_WIDTH // 2
A_HEADS = A_WIDTH // HEAD_DIM
B_WIDTH = MIX_WIDTH // 4
C_WIDTH = MIX_WIDTH - A_WIDTH - B_WIDTH
C_HEADS = C_WIDTH // HEAD_DIM
CONV_WIDTH = 31
CHUNK = 128
Q_BLOCK = 128
D_FF = ((8 * D_MODEL // 3 + 255) // 256) * 256
N_MOD = 6
IN_WIDTH = 3 * A_WIDTH + 2 * B_WIDTH + 2 * C_WIDTH
SB_BIAS_INIT = -8.0
EPS = 1e-6

kernel_name = "hybrid_stickbreak_conformer_gmlp_step"


def rms_norm(x, g):
    xf = x.astype(jnp.float32)
    y = xf * lax.rsqrt(jnp.mean(xf * xf, axis=-1, keepdims=True) + EPS)
    return (y * g.astype(jnp.float32)).astype(x.dtype)


def layer_norm(x, g, b):
    xf = x.astype(jnp.float32)
    mu = jnp.mean(xf, axis=-1, keepdims=True)
    xc = xf - mu
    y = xc * lax.rsqrt(jnp.mean(xc * xc, axis=-1, keepdims=True) + EPS)
    return (y * g.astype(jnp.float32) + b.astype(jnp.float32)).astype(x.dtype)


def stick_breaking_attention(q, k, v, q_pos, k_pos, sb_bias):
    B, Tq, H, HD = q.shape
    blk = min(Q_BLOCK, Tq)
    nb = -(-Tq // blk)
    pad = nb * blk - Tq
    qp = jnp.pad(q, ((0, 0), (0, pad), (0, 0), (0, 0)))
    posp = jnp.pad(q_pos, (0, pad))
    qb = qp.reshape(B, nb, blk, H, HD).transpose(1, 0, 2, 3, 4)
    pb = posp.reshape(nb, blk)
    scale = HD ** -0.5
    bias = sb_bias.astype(jnp.float32)[None, :, None, None]

    def one_block(args):
        qi, pi = args
        z = jnp.einsum('bqhd,bkhd->bhqk', qi, k, preferred_element_type=jnp.float32) * scale + bias
        mask = (k_pos[None, :] < pi[:, None])[None, None]
        log_beta = jax.nn.log_sigmoid(z)
        log_keep = jnp.where(mask, jax.nn.log_sigmoid(-z), 0.0)
        shifted = jnp.concatenate([log_keep[..., 1:], jnp.zeros_like(log_keep[..., :1])], axis=-1)
        log_rest = lax.cumsum(shifted, axis=3, reverse=True)
        w = jnp.where(mask, jnp.exp(log_beta + log_rest), 0.0)
        return jnp.einsum('bhqk,bkhd->bqhd', w.astype(v.dtype), v)

    ob = lax.map(one_block, (qb, pb))
    return ob.transpose(1, 0, 2, 3, 4).reshape(B, nb * blk, H, HD)[:, :Tq]


def conv_module(glu_in, buf, conv_w, conv_b, ln_g, ln_b):
    a, g = jnp.split(glu_in, 2, axis=-1)
    xg = a * jax.nn.sigmoid(g)
    xp = jnp.concatenate([buf.astype(xg.dtype), xg], axis=1)
    y = lax.conv_general_dilated(xp, conv_w[:, None, :], window_strides=(1,), padding='VALID',
                                 dimension_numbers=('NWC', 'WIO', 'NWC'),
                                 feature_group_count=B_WIDTH) + conv_b
    y = jax.nn.silu(layer_norm(y, ln_g, ln_b))
    return y, xp[:, -(CONV_WIDTH - 1):]


def chunk_mlp(u, v_raw, norm_g, w_s, b_s):
    B, T, _ = u.shape
    v = rms_norm(v_raw, norm_g)
    nc = -(-T // CHUNK)
    pad = nc * CHUNK - T
    vp = jnp.pad(v, ((0, 0), (0, pad), (0, 0))).reshape(B, nc, CHUNK, C_HEADS, HEAD_DIM)
    w_causal = w_s * jnp.tril(jnp.ones((CHUNK, CHUNK), w_s.dtype))
    mixed = jnp.einsum('hts,bcshd->bcthd', w_causal, vp) + b_s.T[None, None, :, :, None]
    mixed = mixed.reshape(B, nc * CHUNK, C_WIDTH)[:, :T]
    return u * mixed, v


def decoder_layer(x, c, conv_buf, k_past, v_past,
                  norm1_g, norm2_g, w_ada, b_ada, w_in, q_norm_g, k_norm_g, sb_bias,
                  conv_w, conv_b, conv_ln_g, conv_ln_b, sgu_norm_g, sgu_w, sgu_b,
                  w_out, w_gate, w_up, w_down):
    B, T, _ = x.shape
    mod = (jax.nn.silu(c) @ w_ada + b_ada)[:, None, :]
    shift1, scale1, gate1, shift2, scale2, gate2 = jnp.split(mod, N_MOD, axis=-1)

    h = rms_norm(x, norm1_g) * (1 + scale1) + shift1
    proj = h @ w_in
    o1 = A_WIDTH
    o2 = 2 * A_WIDTH
    o3 = 3 * A_WIDTH
    o4 = o3 + 2 * B_WIDTH
    o5 = o4 + C_WIDTH
    q, k, v, glu_in, u, v_c = jnp.split(proj, [o1, o2, o3, o4, o5], axis=-1)
    q = rms_norm(q.reshape(B, T, A_HEADS, HEAD_DIM), q_norm_g)
    k = rms_norm(k.reshape(B, T, A_HEADS, HEAD_DIM), k_norm_g)
    v = v.reshape(B, T, A_HEADS, HEAD_DIM)

    if k_past is None:
        past_len = 0
        k_all, v_all = k, v
    else:
        past_len = k_past.shape[1]
        k_all = jnp.concatenate([k_past.astype(k.dtype), k], axis=1)
        v_all = jnp.concatenate([v_past.astype(v.dtype), v], axis=1)
    q_pos = past_len + jnp.arange(T, dtype=jnp.int32)
    k_pos = jnp.arange(past_len + T, dtype=jnp.int32)

    o_a = stick_breaking_attention(q, k_all, v_all, q_pos, k_pos, sb_bias).reshape(B, T, A_WIDTH)
    o_b, new_buf = conv_module(glu_in, conv_buf, conv_w, conv_b, conv_ln_g, conv_ln_b)
    o_c, v_rows = chunk_mlp(u, v_c, sgu_norm_g, sgu_w, sgu_b)

    x = x + gate1 * (jnp.concatenate([o_a, o_b, o_c], axis=-1) @ w_out)
    h2 = rms_norm(x, norm2_g) * (1 + scale2) + shift2
    x = x + gate2 * ((jax.nn.silu(h2 @ w_gate) * (h2 @ w_up)) @ w_down)
    return x, k, v, new_buf, v_rows


def setup_inputs(seed: int = 0) -> dict:
    key = jax.random.key(seed)
    ks = jax.random.split(key, 32)
    n_pages = PAST_LEN // PAGE_SIZE
    n_pool = (DEC_BATCH * n_pages * 5) // 4

    def nrm(k, shape, s):
        return jax.random.normal(k, shape, jnp.float32) * s

    page_table = jax.random.permutation(ks[5], n_pool)[:DEC_BATCH * n_pages]
    page_table = page_table.reshape(DEC_BATCH, n_pages).astype(jnp.int32)
    return {
        'x_prompt': nrm(ks[0], (BATCH, SEQ, D_MODEL), 1.0),
        'x_sample': nrm(ks[1], (DEC_BATCH, DEC_SEQ, D_MODEL), 1.0),
        'cache_k': nrm(ks[2], (DEPTH, n_pool, PAGE_SIZE, A_HEADS, HEAD_DIM), 1.0),
        'cache_v': nrm(ks[3], (DEPTH, n_pool, PAGE_SIZE, A_HEADS, HEAD_DIM), 1.0),
        'state_conv': nrm(ks[4], (DEPTH, DEC_BATCH, CONV_WIDTH - 1, B_WIDTH), 0.5),
        'page_table': page_table,
        'c_prompt': nrm(ks[6], (BATCH, D_MODEL), 1.0),
        'c_sample': nrm(ks[7], (DEC_BATCH, D_MODEL), 1.0),
        'norm1_g': 1.0 + nrm(ks[8], (DEPTH, D_MODEL), 0.02),
        'norm2_g': 1.0 + nrm(ks[9], (DEPTH, D_MODEL), 0.02),
        'w_ada': nrm(ks[10], (DEPTH, D_MODEL, N_MOD * D_MODEL), 0.5 * D_MODEL ** -0.5),
        'b_ada': nrm(ks[11], (DEPTH, N_MOD * D_MODEL), 0.01),
        'w_in': nrm(ks[12], (DEPTH, D_MODEL, IN_WIDTH), D_MODEL ** -0.5),
        'q_norm_g': 1.0 + nrm(ks[13], (DEPTH, HEAD_DIM), 0.02),
        'k_norm_g': 1.0 + nrm(ks[14], (DEPTH, HEAD_DIM), 0.02),
        'sb_bias': SB_BIAS_INIT + nrm(ks[26], (DEPTH, A_HEADS), 0.5),
        'conv_w': nrm(ks[15], (DEPTH, CONV_WIDTH, B_WIDTH), CONV_WIDTH ** -0.5),
        'conv_b': nrm(ks[16], (DEPTH, B_WIDTH), 0.01),
        'conv_ln_g': 1.0 + nrm(ks[17], (DEPTH, B_WIDTH), 0.02),
        'conv_ln_b': nrm(ks[18], (DEPTH, B_WIDTH), 0.01),
        'sgu_norm_g': 1.0 + nrm(ks[19], (DEPTH, C_WIDTH), 0.02),
        'sgu_w': nrm(ks[20], (DEPTH, C_HEADS, CHUNK, CHUNK), CHUNK ** -0.5),
        'sgu_b': 1.0 + nrm(ks[21], (DEPTH, C_HEADS, CHUNK), 0.02),
        'w_out': nrm(ks[22], (DEPTH, MIX_WIDTH, D_MODEL), MIX_WIDTH ** -0.5),
        'w_gate': nrm(ks[23], (DEPTH, D_MODEL, D_FF), D_MODEL ** -0.5),
        'w_up': nrm(ks[24], (DEPTH, D_MODEL, D_FF), D_MODEL ** -0.5),
        'w_down': nrm(ks[25], (DEPTH, D_FF, D_MODEL), D_FF ** -0.5),
    }


def reference(x_prompt, x_sample, cache_k, cache_v, state_conv, page_table, c_prompt, c_sample,
              norm1_g, norm2_g, w_ada, b_ada, w_in, q_norm_g, k_norm_g, sb_bias,
              conv_w, conv_b, conv_ln_g, conv_ln_b, sgu_norm_g, sgu_w, sgu_b,
              w_out, w_gate, w_up, w_down):
    b_p = x_prompt.shape[0]
    b_s = x_sample.shape[0]
    n_pages = page_table.shape[1]
    y_p, y_s = x_prompt, x_sample
    kp_rows, vp_rows, ks_rows, vs_rows = [], [], [], []
    conv_p, conv_s, sgu_s = [], [], []
    for l in range(DEPTH):
        lp = (norm1_g[l], norm2_g[l], w_ada[l], b_ada[l], w_in[l], q_norm_g[l], k_norm_g[l], sb_bias[l],
              conv_w[l], conv_b[l], conv_ln_g[l], conv_ln_b[l], sgu_norm_g[l], sgu_w[l], sgu_b[l],
              w_out[l], w_gate[l], w_up[l], w_down[l])
        zero_buf = jnp.zeros((b_p, CONV_WIDTH - 1, B_WIDTH), x_prompt.dtype)
        y_p, k_new, v_new, buf_new, _ = decoder_layer(y_p, c_prompt, zero_buf, None, None, *lp)
        kp_rows.append(k_new)
        vp_rows.append(v_new)
        conv_p.append(buf_new)
        k_past = cache_k[l][page_table].reshape(b_s, n_pages * PAGE_SIZE, A_HEADS, HEAD_DIM)
        v_past = cache_v[l][page_table].reshape(b_s, n_pages * PAGE_SIZE, A_HEADS, HEAD_DIM)
        y_s, k_new, v_new, buf_new, v_rows = decoder_layer(y_s, c_sample, state_conv[l], k_past, v_past, *lp)
        ks_rows.append(k_new)
        vs_rows.append(v_new)
        conv_s.append(buf_new)
        sgu_s.append(v_rows)
    return (y_p, y_s, jnp.stack(kp_rows), jnp.stack(vp_rows), jnp.stack(ks_rows), jnp.stack(vs_rows),
            jnp.stack(conv_p), jnp.stack(conv_s), jnp.stack(sgu_s))
```

```python
import functools

import jax
import jax.numpy as jnp
from jax import lax
from jax.experimental import pallas as pl
from jax.experimental.pallas import tpu as pltpu

F32 = jnp.float32
BF16 = jnp.bfloat16

D_MODEL = 2048
DEPTH = 4
PAGE_SIZE = 128
HEAD_DIM = 128
A_WIDTH = D_MODEL // 2
A_HEADS = A_WIDTH // HEAD_DIM
B_WIDTH = D_MODEL // 4
C_WIDTH = D_MODEL - A_WIDTH - B_WIDTH
C_HEADS = C_WIDTH // HEAD_DIM
CONV_WIDTH = 31
CHUNK = 128
D_FF = ((8 * D_MODEL // 3 + 255) // 256) * 256
N_MOD = 6
EPS = 1e-6

COL_Q = 0
COL_K = A_WIDTH
COL_V = 2 * A_WIDTH
COL_GLU_A = 3 * A_WIDTH
COL_GLU_G = COL_GLU_A + B_WIDTH
COL_U = COL_GLU_G + B_WIDTH
COL_VC = COL_U + C_WIDTH

LANES = 128
SUBLANES = 8
VMEM_LIMIT = 56 * 1024 * 1024
TN = 512
ATT_BLK = 256
CONV_TT = 256
CONV_HALO = 32
SGU_TT = 512
PAGES_PER_STEP = 4


def _cparams(sem):
    return pltpu.CompilerParams(dimension_semantics=sem, vmem_limit_bytes=VMEM_LIMIT)


def _softplus_parts(z):
    t = jnp.log1p(jnp.exp(-jnp.abs(z)))
    return jnp.minimum(z, 0.0) - t, -jnp.maximum(z, 0.0) - t


def _silu(x):
    return x * jax.nn.sigmoid(x)


def _ada_body(c_ref, w_ref, b_ref, o_ref):
    s = _silu(c_ref[...]).astype(BF16)
    o_ref[0] = jnp.dot(s, w_ref[...].astype(BF16), preferred_element_type=F32) + b_ref[0]


def _ada_mod(c_all, w_ada, b_ada):
    rows = c_all.shape[0]
    tn = 1024
    n_mod = w_ada.shape[2]
    return pl.pallas_call(
        _ada_body,
        grid=(DEPTH, n_mod // tn),
        in_specs=[
            pl.BlockSpec((rows, D_MODEL), lambda l, j: (0, 0)),
            pl.BlockSpec((None, D_MODEL, tn), lambda l, j: (l, 0, j)),
            pl.BlockSpec((1, 1, tn), lambda l, j: (l, 0, j)),
        ],
        out_specs=pl.BlockSpec((1, rows, tn), lambda l, j: (l, 0, j)),
        out_shape=jax.ShapeDtypeStruct((DEPTH, rows, n_mod), F32),
        compiler_params=_cparams(("arbitrary", "arbitrary")),
        name="ada_mod",
    )(c_all, w_ada, b_ada.reshape(DEPTH, 1, n_mod))


def _norm_mod_body(x_ref, g_ref, shift_ref, scale_ref, o_ref):
    x = x_ref[0]
    y = x * lax.rsqrt(jnp.mean(x * x, axis=-1, keepdims=True) + EPS)
    y = y * g_ref[...]
    o_ref[0] = (y * (1.0 + scale_ref[0]) + shift_ref[0]).astype(o_ref.dtype)


def _norm_mod(x, g, mod, shift_chunk, scale_chunk, tm):
    G, R, D = x.shape
    rm = mod.shape[1]
    return pl.pallas_call(
        _norm_mod_body,
        grid=(G, R // tm),
        in_specs=[
            pl.BlockSpec((1, tm, D), lambda g_, i: (g_, i, 0)),
            pl.BlockSpec((1, D), lambda g_, i: (0, 0)),
            pl.BlockSpec((1, rm, D), lambda g_, i: (g_, 0, shift_chunk)),
            pl.BlockSpec((1, rm, D), lambda g_, i: (g_, 0, scale_chunk)),
        ],
        out_specs=pl.BlockSpec((1, tm, D), lambda g_, i: (g_, i, 0)),
        out_shape=jax.ShapeDtypeStruct((G, R, D), BF16),
        compiler_params=_cparams(("arbitrary", "arbitrary")),
        name="norm_mod",
    )(x, g.reshape(1, D), mod, mod)


def _first_row_tile():
    return jnp.logical_and(pl.program_id(1) == 0, pl.program_id(2) == 0)


def _proj_body(*refs, kind, n_w, tn):
    h_ref = refs[0]
    w_refs = refs[1:1 + n_w]
    has_gain = kind in ("headnorm", "rmsnorm")
    gain_ref = refs[1 + n_w] if has_gain else None
    o_ref = refs[1 + n_w + int(has_gain)]
    wbf_refs = refs[2 + n_w + int(has_gain):]

    @pl.when(_first_row_tile())
    def _():
        for w_ref, wbf in zip(w_refs, wbf_refs):
            wbf[...] = w_ref[...].astype(BF16)

    h = h_ref[0]
    accs = [jnp.dot(h, wbf[...], preferred_element_type=F32) for wbf in wbf_refs]
    if kind == "plain":
        out = accs[0]
    elif kind == "glu":
        out = accs[0] * jax.nn.sigmoid(accs[1])
    elif kind == "headnorm":
        parts = []
        for hh in range(tn // HEAD_DIM):
            a = accs[0][:, hh * HEAD_DIM:(hh + 1) * HEAD_DIM]
            parts.append(a * lax.rsqrt(jnp.mean(a * a, axis=-1, keepdims=True) + EPS))
        out = jnp.concatenate(parts, axis=-1) * gain_ref[...]
    else:
        a = accs[0]
        out = a * lax.rsqrt(jnp.mean(a * a, axis=-1, keepdims=True) + EPS) * gain_ref[...]
    o_ref[0] = out.astype(o_ref.dtype)


def _proj(h, w, layer, col_starts, ncols, kind, out_dtype, tm, gain=None, name="proj"):
    G, R, K = h.shape
    tn = TN
    n_w = len(col_starts)
    in_specs = [pl.BlockSpec((1, tm, K), lambda j, g_, i: (g_, i, 0))]
    args = [h]
    for cs in col_starts:
        cb = cs // tn
        in_specs.append(pl.BlockSpec((None, K, tn), lambda j, g_, i, cb=cb: (layer, 0, cb + j)))
        args.append(w)
    if gain is not None:
        in_specs.append(pl.BlockSpec((1, tn), lambda j, g_, i: (0, j)))
        args.append(gain.reshape(1, ncols))
    return pl.pallas_call(
        functools.partial(_proj_body, kind=kind, n_w=n_w, tn=tn),
        grid=(ncols // tn, G, R // tm),
        in_specs=in_specs,
        out_specs=pl.BlockSpec((1, tm, tn), lambda j, g_, i: (g_, i, j)),
        out_shape=jax.ShapeDtypeStruct((G, R, ncols), out_dtype),
        scratch_shapes=[pltpu.VMEM((K, tn), BF16) for _ in range(n_w)],
        compiler_params=_cparams(("arbitrary", "arbitrary", "arbitrary")),
        name=name,
    )(*args)


def _attn_body(bias_ref, q_ref, k_ref, v_ref, u_ref, o_ref, *, layer, blk):
    hh = pl.program_id(1)
    qi = pl.program_id(2)
    bias = bias_ref[layer, hh]
    scale = HEAD_DIM ** -0.5
    q = q_ref[0]
    row = lax.broadcasted_iota(jnp.int32, (blk, blk), 0)
    col = lax.broadcasted_iota(jnp.int32, (blk, blk), 1)
    causal = col < row

    def kv_block(j, acc, run, masked):
        ks = pl.multiple_of(j * blk, blk)
        kb = k_ref[0, pl.ds(ks, blk), :].astype(BF16)
        vb = v_ref[0, pl.ds(ks, blk), :].astype(BF16)
        z = lax.dot_general(q, kb, (((1,), (1,)), ((), ())), preferred_element_type=F32) * scale + bias
        lb, lk = _softplus_parts(z)
        if masked:
            lk = jnp.where(causal, lk, 0.0)
        hi = lk.astype(BF16)
        lo = (lk - hi.astype(F32)).astype(BF16)
        rest = jnp.dot(jnp.concatenate([hi, lo], axis=-1), u_ref[...], preferred_element_type=F32)
        w = jnp.exp(lb + rest + run)
        if masked:
            w = jnp.where(causal, w, 0.0)
        acc = acc + jnp.dot(w.astype(BF16), vb, preferred_element_type=F32)
        run = run + jnp.sum(lk, axis=-1, keepdims=True)
        return acc, run

    acc, run = kv_block(qi, jnp.zeros((blk, HEAD_DIM), F32), jnp.zeros((blk, 1), F32), True)
    acc, run = lax.fori_loop(0, qi, lambda it, c: kv_block(qi - 1 - it, c[0], c[1], False), (acc, run))
    o_ref[0] = acc.astype(o_ref.dtype)


def _suffix_matrix(n):
    j = jnp.arange(2 * n)[:, None] % n
    s = jnp.arange(n)[None, :]
    return (j > s).astype(BF16)


def _prompt_attention(q, k, v, sb_bias, layer):
    G, R, _ = q.shape
    blk = ATT_BLK
    return pl.pallas_call(
        functools.partial(_attn_body, layer=layer, blk=blk),
        grid=(G, A_HEADS, R // blk),
        in_specs=[
            pl.BlockSpec(memory_space=pltpu.SMEM),
            pl.BlockSpec((1, blk, HEAD_DIM), lambda b, h, i: (b, i, h)),
            pl.BlockSpec((1, R, HEAD_DIM), lambda b, h, i: (b, 0, h)),
            pl.BlockSpec((1, R, HEAD_DIM), lambda b, h, i: (b, 0, h)),
            pl.BlockSpec((2 * blk, blk), lambda b, h, i: (0, 0)),
        ],
        out_specs=pl.BlockSpec((1, blk, HEAD_DIM), lambda b, h, i: (b, i, h)),
        out_shape=jax.ShapeDtypeStruct((G, R, A_WIDTH), BF16),
        compiler_params=_cparams(("arbitrary", "arbitrary", "arbitrary")),
        name="prompt_attn",
    )(sb_bias, q, k, v, _suffix_matrix(blk))


def _conv_body(cur_ref, halo_ref, cw_ref, cb_ref, lg_ref, lb_ref, o_ref, xp_ref, *, tt):
    i = pl.program_id(1)
    xp_ref[0:CONV_HALO, :] = jnp.where(i > 0, halo_ref[0], 0.0)
    xp_ref[CONV_HALO:, :] = cur_ref[0]
    first = CONV_HALO - (CONV_WIDTH - 1)
    y = cb_ref[...] + cw_ref[0:1, :] * xp_ref[pl.ds(first, tt), :]
    for w in range(1, CONV_WIDTH):
        y = y + cw_ref[w:w + 1, :] * xp_ref[pl.ds(first + w, tt), :]
    mu = jnp.mean(y, axis=-1, keepdims=True)
    yc = y - mu
    yn = yc * lax.rsqrt(jnp.mean(yc * yc, axis=-1, keepdims=True) + EPS)
    o_ref[0] = _silu(yn * lg_ref[...] + lb_ref[...]).astype(o_ref.dtype)


def _prompt_conv(xg, conv_w, conv_b, ln_g, ln_b):
    G, R, C = xg.shape
    tt = CONV_TT
    hb = tt // CONV_HALO
    row = lambda a: a.reshape(1, C)
    return pl.pallas_call(
        functools.partial(_conv_body, tt=tt),
        grid=(G, R // tt),
        in_specs=[
            pl.BlockSpec((1, tt, C), lambda b, i: (b, i, 0)),
            pl.BlockSpec((1, CONV_HALO, C), lambda b, i: (b, jnp.maximum(i * hb - 1, 0), 0)),
            pl.BlockSpec((CONV_WIDTH, C), lambda b, i: (0, 0)),
            pl.BlockSpec((1, C), lambda b, i: (0, 0)),
            pl.BlockSpec((1, C), lambda b, i: (0, 0)),
            pl.BlockSpec((1, C), lambda b, i: (0, 0)),
        ],
        out_specs=pl.BlockSpec((1, tt, C), lambda b, i: (b, i, 0)),
        out_shape=jax.ShapeDtypeStruct((G, R, C), BF16),
        scratch_shapes=[pltpu.VMEM((tt + CONV_HALO, C), F32)],
        compiler_params=_cparams(("arbitrary", "arbitrary")),
        name="prompt_conv",
    )(xg, xg, conv_w, row(conv_b), row(ln_g), row(ln_b))


def _sgu_body(u_ref, v_ref, w_ref, b_ref, o_ref, *, tt):
    t = lax.broadcasted_iota(jnp.int32, (CHUNK, CHUNK), 0)
    s = lax.broadcasted_iota(jnp.int32, (CHUNK, CHUNK), 1)
    for hh in range(C_HEADS):
        wc = jnp.where(s <= t, w_ref[hh], 0.0).astype(BF16)
        cols = slice(hh * HEAD_DIM, (hh + 1) * HEAD_DIM)
        for c in range(tt // CHUNK):
            rows = slice(c * CHUNK, (c + 1) * CHUNK)
            mixed = jnp.dot(wc, v_ref[0, rows, cols].astype(BF16), preferred_element_type=F32) + b_ref[:, cols]
            o_ref[0, rows, cols] = (u_ref[0, rows, cols] * mixed).astype(o_ref.dtype)


def _prompt_sgu(u, vn, sgu_w, bias_exp):
    G, R, C = u.shape
    tt = SGU_TT
    return pl.pallas_call(
        functools.partial(_sgu_body, tt=tt),
        grid=(G, R // tt),
        in_specs=[
            pl.BlockSpec((1, tt, C), lambda b, i: (b, i, 0)),
            pl.BlockSpec((1, tt, C), lambda b, i: (b, i, 0)),
            pl.BlockSpec((C_HEADS, CHUNK, CHUNK), lambda b, i: (0, 0, 0)),
            pl.BlockSpec((CHUNK, C), lambda b, i: (0, 0)),
        ],
        out_specs=pl.BlockSpec((1, tt, C), lambda b, i: (b, i, 0)),
        out_shape=jax.ShapeDtypeStruct((G, R, C), BF16),
        compiler_params=_cparams(("arbitrary", "arbitrary")),
        name="prompt_sgu",
    )(u, vn, sgu_w, bias_exp)


def _out_proj_body(oa_ref, ob_ref, oc_ref, w_ref, x_ref, gate_ref, o_ref, wbf_ref):
    @pl.when(_first_row_tile())
    def _():
        wbf_ref[...] = w_ref[...].astype(BF16)

    acc = jnp.dot(oa_ref[0], wbf_ref[0:A_WIDTH, :], preferred_element_type=F32)
    acc = acc + jnp.dot(ob_ref[0], wbf_ref[A_WIDTH:A_WIDTH + B_WIDTH, :], preferred_element_type=F32)
    acc = acc + jnp.dot(oc_ref[0], wbf_ref[A_WIDTH + B_WIDTH:, :], preferred_element_type=F32)
    o_ref[0] = x_ref[0] + gate_ref[0] * acc


def _out_proj(oa, ob, oc, w_out, layer, x, mod, gate_chunk, tm):
    G, R, D = x.shape
    rm = mod.shape[1]
    tn = TN
    gb = gate_chunk * (D // tn)
    act = lambda width: pl.BlockSpec((1, tm, width), lambda j, g_, i: (g_, i, 0))
    return pl.pallas_call(
        _out_proj_body,
        grid=(D // tn, G, R // tm),
        in_specs=[
            act(A_WIDTH), act(B_WIDTH), act(C_WIDTH),
            pl.BlockSpec((None, D, tn), lambda j, g_, i: (layer, 0, j)),
            pl.BlockSpec((1, tm, tn), lambda j, g_, i: (g_, i, j)),
            pl.BlockSpec((1, rm, tn), lambda j, g_, i: (g_, 0, gb + j)),
        ],
        out_specs=pl.BlockSpec((1, tm, tn), lambda j, g_, i: (g_, i, j)),
        out_shape=jax.ShapeDtypeStruct((G, R, D), F32),
        scratch_shapes=[pltpu.VMEM((D, tn), BF16)],
        compiler_params=_cparams(("arbitrary", "arbitrary", "arbitrary")),
        name="out_proj",
    )(oa, ob, oc, w_out, x, mod)


def _gate_up_body(h_ref, wg_ref, wu_ref, o_ref, wg_bf, wu_bf):
    @pl.when(_first_row_tile())
    def _():
        wg_bf[...] = wg_ref[...].astype(BF16)
        wu_bf[...] = wu_ref[...].astype(BF16)

    h = h_ref[0]
    a = jnp.dot(h, wg_bf[...], preferred_element_type=F32)
    b = jnp.dot(h, wu_bf[...], preferred_element_type=F32)
    o_ref[0] = (_silu(a) * b).astype(o_ref.dtype)


def _gate_up(h, w_gate, w_up, layer, tm):
    G, R, K = h.shape
    tn = TN
    wspec = pl.BlockSpec((None, K, tn), lambda j, g_, i: (layer, 0, j))
    return pl.pallas_call(
        _gate_up_body,
        grid=(D_FF // tn, G, R // tm),
        in_specs=[pl.BlockSpec((1, tm, K), lambda j, g_, i: (g_, i, 0)), wspec, wspec],
        out_specs=pl.BlockSpec((1, tm, tn), lambda j, g_, i: (g_, i, j)),
        out_shape=jax.ShapeDtypeStruct((G, R, D_FF), BF16),
        scratch_shapes=[pltpu.VMEM((K, tn), BF16), pltpu.VMEM((K, tn), BF16)],
        compiler_params=_cparams(("arbitrary", "arbitrary", "arbitrary")),
        name="ffn_gate_up",
    )(h, w_gate, w_up)


def _down_body(h_ref, w_ref, x_ref, gate_ref, o_ref, wbf_ref):
    @pl.when(_first_row_tile())
    def _():
        wbf_ref[...] = w_ref[...].astype(BF16)

    acc = jnp.dot(h_ref[0], wbf_ref[...], preferred_element_type=F32)
    o_ref[0] = x_ref[0] + gate_ref[0] * acc


def _down(hid, w_down, layer, x, mod, gate_chunk, tm):
    G, R, D = x.shape
    K = hid.shape[2]
    rm = mod.shape[1]
    tn = TN
    gb = gate_chunk * (D // tn)
    return pl.pallas_call(
        _down_body,
        grid=(D // tn, G, R // tm),
        in_specs=[
            pl.BlockSpec((1, tm, K), lambda j, g_, i: (g_, i, 0)),
            pl.BlockSpec((None, K, tn), lambda j, g_, i: (layer, 0, j)),
            pl.BlockSpec((1, tm, tn), lambda j, g_, i: (g_, i, j)),
            pl.BlockSpec((1, rm, tn), lambda j, g_, i: (g_, 0, gb + j)),
        ],
        out_specs=pl.BlockSpec((1, tm, tn), lambda j, g_, i: (g_, i, j)),
        out_shape=jax.ShapeDtypeStruct((G, R, D), F32),
        scratch_shapes=[pltpu.VMEM((K, tn), BF16)],
        compiler_params=_cparams(("arbitrary", "arbitrary", "arbitrary")),
        name="ffn_down",
    )(hid, w_down, x, mod)


def _paged_attn_body(pt_ref, q_ref, bias_ref, u_ref, *refs, n_steps):
    del pt_ref
    P = PAGES_PER_STEP
    k_refs = refs[:P]
    v_refs = refs[P:2 * P]
    o_ref = refs[2 * P]
    acc_ref, run_ref = refs[2 * P + 1:]
    t = pl.program_id(1)
    rows = PAGE_SIZE * A_HEADS
    n_blk = rows // LANES

    @pl.when(t == 0)
    def _():
        acc_ref[...] = jnp.zeros_like(acc_ref)
        run_ref[...] = jnp.zeros_like(run_ref)

    scale = HEAD_DIM ** -0.5
    q = q_ref[0]
    lane = lax.broadcasted_iota(jnp.int32, (A_HEADS, rows), 1)
    head = lax.broadcasted_iota(jnp.int32, (A_HEADS, rows), 0)
    own = (lane % A_HEADS) == head
    bias = bias_ref[...]
    acc = acc_ref[...]
    run = run_ref[...]
    for p in range(P):
        kp = k_refs[p][...].astype(BF16)
        vp = v_refs[p][...].astype(BF16)
        z = lax.dot_general(q, kp, (((1,), (1,)), ((), ())), preferred_element_type=F32) * scale + bias
        lb, lk = _softplus_parts(z)
        lk = jnp.where(own, lk, 0.0)
        stacked = jnp.concatenate([lk[:, c * LANES:(c + 1) * LANES] for c in range(n_blk)], axis=0)
        hi = stacked.astype(BF16)
        lo = (stacked - hi.astype(F32)).astype(BF16)
        r = jnp.dot(jnp.concatenate([hi, lo], axis=-1), u_ref[...], preferred_element_type=F32)
        rest = [None] * n_blk
        for c in reversed(range(n_blk)):
            blk_rows = slice(c * A_HEADS, (c + 1) * A_HEADS)
            rest[c] = r[blk_rows, :LANES] + run
            run = run + r[blk_rows, LANES:]
        w = jnp.where(own, jnp.exp(lb + jnp.concatenate(rest, axis=-1)), 0.0)
        acc = acc + jnp.dot(w.astype(BF16), vp, preferred_element_type=F32)
    acc_ref[...] = acc
    run_ref[...] = run

    @pl.when(t == n_steps - 1)
    def _():
        o_ref[0] = acc


def _paged_suffix_matrix():
    j = jnp.arange(2 * LANES)[:, None] % LANES
    n = jnp.arange(2 * LANES)[None, :]
    return jnp.logical_or(n >= LANES, j > n).astype(BF16)


def _paged_attention(q, cache_k, cache_v, page_table, sb_bias_l, layer):
    B, n_pages = page_table.shape
    P = PAGES_PER_STEP
    n_steps = n_pages // P
    rows = PAGE_SIZE * A_HEADS

    def page_spec(p):
        return pl.BlockSpec((None, None, rows, HEAD_DIM),
                            lambda b, t, pt: (layer, pt[b, n_pages - 1 - (t * P + p)], 0, 0))

    grid_spec = pltpu.PrefetchScalarGridSpec(
        num_scalar_prefetch=1,
        grid=(B, n_steps),
        in_specs=[
            pl.BlockSpec((1, A_HEADS, HEAD_DIM), lambda b, t, pt: (b, 0, 0)),
            pl.BlockSpec((A_HEADS, 1), lambda b, t, pt: (0, 0)),
            pl.BlockSpec((2 * LANES, 2 * LANES), lambda b, t, pt: (0, 0)),
        ] + [page_spec(p) for p in range(P)] * 2,
        out_specs=pl.BlockSpec((1, A_HEADS, HEAD_DIM), lambda b, t, pt: (b, 0, 0)),
        scratch_shapes=[pltpu.VMEM((A_HEADS, HEAD_DIM), F32), pltpu.VMEM((A_HEADS, LANES), F32)],
    )
    return pl.pallas_call(
        functools.partial(_paged_attn_body, n_steps=n_steps),
        grid_spec=grid_spec,
        out_shape=jax.ShapeDtypeStruct((B, A_HEADS, HEAD_DIM), F32),
        compiler_params=_cparams(("arbitrary", "arbitrary")),
        name="paged_attn",
    )(page_table, q, sb_bias_l.reshape(A_HEADS, 1), _paged_suffix_matrix(),
      *([cache_k] * P), *([cache_v] * P))


def _sample_mix_body(xg_ref, st_ref, cw_ref, cb_ref, lg_ref, lb_ref, u_ref, vn_ref, w00_ref, b0_ref,
                     ob_ref, oc_ref):
    hist = CONV_WIDTH - 1
    xg = xg_ref[0]
    y = jnp.sum(st_ref[...] * cw_ref[0:hist, :][None], axis=1) + xg * cw_ref[hist:CONV_WIDTH, :] + cb_ref[...]
    mu = jnp.mean(y, axis=-1, keepdims=True)
    yc = y - mu
    yn = yc * lax.rsqrt(jnp.mean(yc * yc, axis=-1, keepdims=True) + EPS)
    ob_ref[0] = _silu(yn * lg_ref[...] + lb_ref[...]).astype(ob_ref.dtype)
    mixed = vn_ref[0].astype(BF16).astype(F32) * w00_ref[...].astype(BF16).astype(F32) + b0_ref[...]
    oc_ref[0] = (u_ref[0] * mixed).astype(oc_ref.dtype)


def _sample_mix(xg, state, conv_w, conv_b, ln_g, ln_b, u, vn, w00e, b0e):
    _, B, C = xg.shape
    row = lambda a: a.reshape(1, C)
    out = jax.ShapeDtypeStruct((1, B, C), BF16)
    return pl.pallas_call(
        _sample_mix_body,
        out_shape=(out, out),
        compiler_params=pltpu.CompilerParams(vmem_limit_bytes=VMEM_LIMIT),
        name="sample_mix",
    )(xg, state, conv_w, row(conv_b), row(ln_g), row(ln_b), u, vn, row(w00e), row(b0e))


def _mixer_inputs(h, w_in, layer, q_gain, k_gain, sgu_gain, tm):
    q = _proj(h, w_in, layer, [COL_Q], A_WIDTH, "headnorm", BF16, tm, gain=q_gain, name="proj_q")
    k = _proj(h, w_in, layer, [COL_K], A_WIDTH, "headnorm", F32, tm, gain=k_gain, name="proj_k")
    v = _proj(h, w_in, layer, [COL_V], A_WIDTH, "plain", F32, tm, name="proj_v")
    xg = _proj(h, w_in, layer, [COL_GLU_A, COL_GLU_G], B_WIDTH, "glu", F32, tm, name="proj_glu")
    u = _proj(h, w_in, layer, [COL_U], C_WIDTH, "plain", F32, tm, name="proj_u")
    vn = _proj(h, w_in, layer, [COL_VC], C_WIDTH, "rmsnorm", F32, tm, gain=sgu_gain, name="proj_vc")
    return q, k, v, xg, u, vn


def _channel_sublayer(x, mod, norm2_g, w_gate, w_up, w_down, layer, tm):
    h2 = _norm_mod(x, norm2_g, mod, 3, 4, tm)
    hid = _gate_up(h2, w_gate, w_up, layer, tm)
    return _down(hid, w_down, layer, x, mod, 5, tm)


def kernel(x_prompt, x_sample, cache_k, cache_v, state_conv, page_table, c_prompt, c_sample, norm1_g, norm2_g, w_ada, b_ada, w_in, q_norm_g, k_norm_g, sb_bias, conv_w, conv_b, conv_ln_g, conv_ln_b, sgu_norm_g, sgu_w, sgu_b, w_out, w_gate, w_up, w_down):
    b_p, seq, _ = x_prompt.shape
    b_s = x_sample.shape[0]
    n_pool = cache_k.shape[1]
    tm_p = 512
    tm_s = b_s

    n_c = b_p + b_s
    pad = (-n_c) % SUBLANES
    c_all = jnp.concatenate([c_prompt, c_sample, jnp.zeros((pad, D_MODEL), F32)], axis=0)
    mod = _ada_mod(c_all, w_ada, b_ada)
    mod_p = mod[:, :b_p].reshape(DEPTH, b_p, 1, N_MOD * D_MODEL)
    mod_s = mod[:, b_p:n_c].reshape(DEPTH, 1, b_s, N_MOD * D_MODEL)

    ck = cache_k.reshape(DEPTH, n_pool, PAGE_SIZE * A_HEADS, HEAD_DIM)
    cv = cache_v.reshape(DEPTH, n_pool, PAGE_SIZE * A_HEADS, HEAD_DIM)

    y_p = x_prompt
    y_s = x_sample.reshape(1, b_s, D_MODEL)
    kp_rows, vp_rows, ks_rows, vs_rows, conv_p, conv_s, sgu_s = [], [], [], [], [], [], []
    for l in range(DEPTH):
        q_gain = jnp.tile(q_norm_g[l], A_HEADS)
        k_gain = jnp.tile(k_norm_g[l], A_HEADS)
        sgu_bias = jnp.repeat(sgu_b[l].T, HEAD_DIM, axis=1)

        h = _norm_mod(y_p, norm1_g[l], mod_p[l], 0, 1, tm_p)
        q, k, v, xg, u, vn = _mixer_inputs(h, w_in, l, q_gain, k_gain, sgu_norm_g[l], tm_p)
        o_a = _prompt_attention(q, k, v, sb_bias, l)
        o_b = _prompt_conv(xg, conv_w[l], conv_b[l], conv_ln_g[l], conv_ln_b[l])
        o_c = _prompt_sgu(u, vn, sgu_w[l], sgu_bias)
        y_p = _out_proj(o_a, o_b, o_c, w_out, l, y_p, mod_p[l], 2, tm_p)
        y_p = _channel_sublayer(y_p, mod_p[l], norm2_g[l], w_gate, w_up, w_down, l, tm_p)
        kp_rows.append(k.reshape(b_p, seq, A_HEADS, HEAD_DIM))
        vp_rows.append(v.reshape(b_p, seq, A_HEADS, HEAD_DIM))
        conv_p.append(xg[:, seq - (CONV_WIDTH - 1):, :])

        h = _norm_mod(y_s, norm1_g[l], mod_s[l], 0, 1, tm_s)
        q, k, v, xg, u, vn = _mixer_inputs(h, w_in, l, q_gain, k_gain, sgu_norm_g[l], tm_s)
        o_a = _paged_attention(q.reshape(b_s, A_HEADS, HEAD_DIM), ck, cv, page_table, sb_bias[l], l)
        o_a = o_a.reshape(1, b_s, A_WIDTH).astype(BF16)
        w00e = jnp.repeat(sgu_w[l][:, 0, 0], HEAD_DIM)
        b0e = jnp.repeat(sgu_b[l][:, 0], HEAD_DIM)
        o_b, o_c = _sample_mix(xg, state_conv[l], conv_w[l], conv_b[l], conv_ln_g[l], conv_ln_b[l],
                               u, vn, w00e, b0e)
        y_s = _out_proj(o_a, o_b, o_c, w_out, l, y_s, mod_s[l], 2, tm_s)
        y_s = _channel_sublayer(y_s, mod_s[l], norm2_g[l], w_gate, w_up, w_down, l, tm_s)
        ks_rows.append(k.reshape(b_s, 1, A_HEADS, HEAD_DIM))
        vs_rows.append(v.reshape(b_s, 1, A_HEADS, HEAD_DIM))
        conv_s.append(jnp.concatenate([state_conv[l][:, 1:, :], xg.reshape(b_s, 1, B_WIDTH)], axis=1))
        sgu_s.append(vn.reshape(b_s, 1, C_WIDTH))

    return (y_p, y_s.reshape(b_s, 1, D_MODEL), jnp.stack(kp_rows), jnp.stack(vp_rows),
            jnp.stack(ks_rows), jnp.stack(vs_rows), jnp.stack(conv_p), jnp.stack(conv_s), jnp.stack(sgu_s))
```

```python
import functools

import jax
import jax.numpy as jnp
from jax import lax
from jax.experimental import pallas as pl
from jax.experimental.pallas import tpu as pltpu

F32 = jnp.float32
BF16 = jnp.bfloat16

D_MODEL = 2048
DEPTH = 4
PAGE_SIZE = 128
HEAD_DIM = 128
A_WIDTH = D_MODEL // 2
A_HEADS = A_WIDTH // HEAD_DIM
B_WIDTH = D_MODEL // 4
C_WIDTH = D_MODEL - A_WIDTH - B_WIDTH
C_HEADS = C_WIDTH // HEAD_DIM
CONV_WIDTH = 31
CHUNK = 128
D_FF = ((8 * D_MODEL // 3 + 255) // 256) * 256
N_MOD = 6
EPS = 1e-6

COL_Q = 0
COL_K = A_WIDTH
COL_V = 2 * A_WIDTH
COL_GLU_A = 3 * A_WIDTH
COL_GLU_G = COL_GLU_A + B_WIDTH
COL_U = COL_GLU_G + B_WIDTH
COL_VC = COL_U + C_WIDTH

LANES = 128
SUBLANES = 8
VMEM_LIMIT = 56 * 1024 * 1024
TM_PROMPT = 1024
TM_DOWN = 512
TN = 512
TN_WIDE = 1024
ATT_BLK = 256
ATT_HEADS = 8
CONV_TT = 256
CONV_HALO = 32
SGU_TT = 512
PAGES_PER_STEP = 8

_NT = (((1,), (1,)), ((), ()))


def _cparams(sem):
    return pltpu.CompilerParams(dimension_semantics=sem, vmem_limit_bytes=VMEM_LIMIT)


def _stick_logs(z):
    a = jnp.maximum(z, 0.0) + jnp.log(1.0 + jnp.exp(-jnp.abs(z)))
    return a, z - a


def _silu(x):
    return x * jax.nn.sigmoid(x)


def _suffix_matrix():
    j = jnp.arange(LANES)[:, None]
    n = jnp.arange(2 * LANES)[None, :]
    return jnp.logical_or(n >= LANES, j > n).astype(BF16)


def _ada_body(c_ref, w_ref, b_ref, o_ref):
    s = _silu(c_ref[...]).astype(BF16)
    o_ref[0] = jnp.dot(s, w_ref[...].astype(BF16), preferred_element_type=F32) + b_ref[0]


def _ada_mod(c_all, w_ada, b_ada):
    rows = c_all.shape[0]
    tn = 1024
    n_mod = w_ada.shape[2]
    return pl.pallas_call(
        _ada_body,
        grid=(DEPTH, n_mod // tn),
        in_specs=[
            pl.BlockSpec((rows, D_MODEL), lambda l, j: (0, 0)),
            pl.BlockSpec((None, D_MODEL, tn), lambda l, j: (l, 0, j)),
            pl.BlockSpec((1, 1, tn), lambda l, j: (l, 0, j)),
        ],
        out_specs=pl.BlockSpec((1, rows, tn), lambda l, j: (l, 0, j)),
        out_shape=jax.ShapeDtypeStruct((DEPTH, rows, n_mod), F32),
        compiler_params=_cparams(("arbitrary", "arbitrary")),
        name="ada_mod",
    )(c_all, w_ada, b_ada.reshape(DEPTH, 1, n_mod))


def _norm_mod_body(x_ref, g_ref, shift_ref, scale_ref, o_ref):
    x = x_ref[0]
    y = x * lax.rsqrt(jnp.mean(x * x, axis=-1, keepdims=True) + EPS)
    y = y * g_ref[...]
    o_ref[0] = (y * (1.0 + scale_ref[0]) + shift_ref[0]).astype(o_ref.dtype)


def _norm_mod(x, g, mod, shift_chunk, scale_chunk, tm):
    G, R, D = x.shape
    rm = mod.shape[1]
    return pl.pallas_call(
        _norm_mod_body,
        grid=(G, R // tm),
        in_specs=[
            pl.BlockSpec((1, tm, D), lambda g_, i: (g_, i, 0)),
            pl.BlockSpec((1, D), lambda g_, i: (0, 0)),
            pl.BlockSpec((1, rm, D), lambda g_, i: (g_, 0, shift_chunk)),
            pl.BlockSpec((1, rm, D), lambda g_, i: (g_, 0, scale_chunk)),
        ],
        out_specs=pl.BlockSpec((1, tm, D), lambda g_, i: (g_, i, 0)),
        out_shape=jax.ShapeDtypeStruct((G, R, D), BF16),
        compiler_params=_cparams(("arbitrary", "arbitrary")),
        name="norm_mod",
    )(x, g.reshape(1, D), mod, mod)


def _first_row_tile(g_axis):
    return jnp.logical_and(pl.program_id(g_axis) == 0, pl.program_id(g_axis + 1) == 0)


def _proj_body(*refs, kind, n_w, n_out):
    h_ref = refs[0]
    w_refs = refs[1:1 + n_w]
    has_gain = kind in ("headnorm", "split_norm")
    gain_ref = refs[1 + n_w] if has_gain else None
    first_out = 1 + n_w + int(has_gain)
    o_refs = refs[first_out:first_out + n_out]
    wbf_refs = refs[first_out + n_out:]

    @pl.when(_first_row_tile(0))
    def _():
        for w_ref, wbf in zip(w_refs, wbf_refs):
            wbf[...] = w_ref[...].astype(BF16)

    h = h_ref[0]
    accs = [jnp.dot(h, wbf[...], preferred_element_type=F32) for wbf in wbf_refs]
    if kind == "plain":
        outs = [accs[0]]
    elif kind == "glu":
        outs = [accs[0] * jax.nn.sigmoid(accs[1])]
    elif kind == "headnorm":
        parts = []
        for hh in range(accs[0].shape[1] // HEAD_DIM):
            a = accs[0][:, hh * HEAD_DIM:(hh + 1) * HEAD_DIM]
            parts.append(a * lax.rsqrt(jnp.mean(a * a, axis=-1, keepdims=True) + EPS))
        outs = [jnp.concatenate(parts, axis=-1) * gain_ref[...]]
    else:
        half = accs[0].shape[1] // 2
        a = accs[0][:, half:]
        outs = [accs[0][:, :half], a * lax.rsqrt(jnp.mean(a * a, axis=-1, keepdims=True) + EPS) * gain_ref[...]]
    for o_ref, out in zip(o_refs, outs):
        o_ref[0] = out.astype(o_ref.dtype)


def _proj(h, w, layer, col_starts, tn, kind, out_widths, out_dtypes, tm, gain=None, name="proj"):
    G, R, K = h.shape
    n_w = len(col_starts)
    in_specs = [pl.BlockSpec((1, tm, K), lambda g_, i: (g_, i, 0))]
    args = [h]
    for cs in col_starts:
        in_specs.append(pl.BlockSpec((None, K, tn), lambda g_, i, cb=cs // tn: (layer, 0, cb)))
        args.append(w)
    if gain is not None:
        in_specs.append(pl.BlockSpec((1, gain.size), lambda g_, i: (0, 0)))
        args.append(gain.reshape(1, gain.size))
    outs = pl.pallas_call(
        functools.partial(_proj_body, kind=kind, n_w=n_w, n_out=len(out_widths)),
        grid=(G, R // tm),
        in_specs=in_specs,
        out_specs=[pl.BlockSpec((1, tm, wd), lambda g_, i: (g_, i, 0)) for wd in out_widths],
        out_shape=[jax.ShapeDtypeStruct((G, R, wd), dt) for wd, dt in zip(out_widths, out_dtypes)],
        scratch_shapes=[pltpu.VMEM((K, tn), BF16) for _ in range(n_w)],
        compiler_params=_cparams(("arbitrary", "arbitrary")),
        name=name,
    )(*args)
    return outs


def _attn_body(bias_ref, q_ref, k_ref, v_ref, u_ref, o_ref, acc_ref, run_ref, *, layer, blk):
    hp = pl.program_id(1)
    qi = pl.program_id(2)
    scale = HEAD_DIM ** -0.5
    n_sub = blk // LANES
    row = lax.broadcasted_iota(jnp.int32, (blk, LANES), 0)
    col = lax.broadcasted_iota(jnp.int32, (blk, LANES), 1)
    acc_ref[...] = jnp.zeros_like(acc_ref)
    run_ref[...] = jnp.zeros_like(run_ref)

    def kv_block(j, masked):
        ks = pl.multiple_of(j * blk, blk)
        for e in range(ATT_HEADS):
            cols = slice(e * HEAD_DIM, (e + 1) * HEAD_DIM)
            q = q_ref[0, :, cols]
            bias = bias_ref[layer, hp * ATT_HEADS + e]
            lbs, rests, tots, valids = [], [], [], []
            for c in range(n_sub):
                kb = k_ref[0, pl.ds(pl.multiple_of(ks + c * LANES, LANES), LANES), cols].astype(BF16)
                z = lax.dot_general(q, kb, _NT, preferred_element_type=F32) * scale + bias
                a, lb = _stick_logs(z)
                if masked:
                    valid = col + c * LANES < row
                    a = jnp.where(valid, a, 0.0)
                    valids.append(valid)
                r = jnp.dot(a.astype(BF16), u_ref[...], preferred_element_type=F32)
                lbs.append(lb)
                rests.append(r[:, :LANES])
                tots.append(r[:, LANES:])
            seen = run_ref[e]
            ws = [None] * n_sub
            for c in reversed(range(n_sub)):
                w = jnp.exp(lbs[c] - (rests[c] + seen))
                if masked:
                    w = jnp.where(valids[c], w, 0.0)
                ws[c] = w.astype(BF16)
                seen = seen + tots[c]
            run_ref[e] = seen
            vb = v_ref[0, pl.ds(ks, blk), cols].astype(BF16)
            acc_ref[e] += jnp.dot(jnp.concatenate(ws, axis=-1), vb, preferred_element_type=F32)

    kv_block(qi, True)

    def body(it, carry):
        kv_block(qi - 1 - it, False)
        return carry

    lax.fori_loop(0, qi, body, 0)
    for e in range(ATT_HEADS):
        o_ref[0, :, e * HEAD_DIM:(e + 1) * HEAD_DIM] = acc_ref[e].astype(o_ref.dtype)


def _prompt_attention(q, k, v, sb_bias, layer):
    G, R, _ = q.shape
    blk = ATT_BLK
    wd = ATT_HEADS * HEAD_DIM
    return pl.pallas_call(
        functools.partial(_attn_body, layer=layer, blk=blk),
        grid=(G, A_HEADS // ATT_HEADS, R // blk),
        in_specs=[
            pl.BlockSpec(memory_space=pltpu.SMEM),
            pl.BlockSpec((1, blk, wd), lambda b, h, i: (b, i, h)),
            pl.BlockSpec((1, R, wd), lambda b, h, i: (b, 0, h)),
            pl.BlockSpec((1, R, wd), lambda b, h, i: (b, 0, h)),
            pl.BlockSpec((LANES, 2 * LANES), lambda b, h, i: (0, 0)),
        ],
        out_specs=pl.BlockSpec((1, blk, wd), lambda b, h, i: (b, i, h)),
        out_shape=jax.ShapeDtypeStruct((G, R, A_WIDTH), BF16),
        scratch_shapes=[pltpu.VMEM((ATT_HEADS, blk, HEAD_DIM), F32), pltpu.VMEM((ATT_HEADS, blk, LANES), F32)],
        compiler_params=_cparams(("arbitrary", "arbitrary", "arbitrary")),
        name="prompt_attn",
    )(sb_bias, q, k, v, _suffix_matrix())


def _conv_body(cur_ref, halo_ref, cw_ref, cb_ref, lg_ref, lb_ref, o_ref, xp_ref, *, tt):
    i = pl.program_id(1)
    xp_ref[0:CONV_HALO, :] = jnp.where(i > 0, halo_ref[0], 0.0)
    xp_ref[CONV_HALO:, :] = cur_ref[0]
    first = CONV_HALO - (CONV_WIDTH - 1)
    y = cb_ref[...] + cw_ref[0:1, :] * xp_ref[pl.ds(first, tt), :]
    for w in range(1, CONV_WIDTH):
        y = y + cw_ref[w:w + 1, :] * xp_ref[pl.ds(first + w, tt), :]
    mu = jnp.mean(y, axis=-1, keepdims=True)
    yc = y - mu
    yn = yc * lax.rsqrt(jnp.mean(yc * yc, axis=-1, keepdims=True) + EPS)
    o_ref[0] = _silu(yn * lg_ref[...] + lb_ref[...]).astype(o_ref.dtype)


def _prompt_conv(xg, conv_w, conv_b, ln_g, ln_b):
    G, R, C = xg.shape
    tt = CONV_TT
    hb = tt // CONV_HALO
    row = lambda a: a.reshape(1, C)
    return pl.pallas_call(
        functools.partial(_conv_body, tt=tt),
        grid=(G, R // tt),
        in_specs=[
            pl.BlockSpec((1, tt, C), lambda b, i: (b, i, 0)),
            pl.BlockSpec((1, CONV_HALO, C), lambda b, i: (b, jnp.maximum(i * hb - 1, 0), 0)),
            pl.BlockSpec((CONV_WIDTH, C), lambda b, i: (0, 0)),
            pl.BlockSpec((1, C), lambda b, i: (0, 0)),
            pl.BlockSpec((1, C), lambda b, i: (0, 0)),
            pl.BlockSpec((1, C), lambda b, i: (0, 0)),
        ],
        out_specs=pl.BlockSpec((1, tt, C), lambda b, i: (b, i, 0)),
        out_shape=jax.ShapeDtypeStruct((G, R, C), BF16),
        scratch_shapes=[pltpu.VMEM((tt + CONV_HALO, C), F32)],
        compiler_params=_cparams(("arbitrary", "arbitrary")),
        name="prompt_conv",
    )(xg, xg, conv_w, row(conv_b), row(ln_g), row(ln_b))


def _sgu_body(u_ref, v_ref, w_ref, b_ref, o_ref, *, tt):
    t = lax.broadcasted_iota(jnp.int32, (CHUNK, CHUNK), 0)
    s = lax.broadcasted_iota(jnp.int32, (CHUNK, CHUNK), 1)
    for hh in range(C_HEADS):
        wc = jnp.where(s <= t, w_ref[hh], 0.0).astype(BF16)
        cols = slice(hh * HEAD_DIM, (hh + 1) * HEAD_DIM)
        for c in range(tt // CHUNK):
            rows = slice(c * CHUNK, (c + 1) * CHUNK)
            mixed = jnp.dot(wc, v_ref[0, rows, cols].astype(BF16), preferred_element_type=F32) + b_ref[:, cols]
            o_ref[0, rows, cols] = (u_ref[0, rows, cols].astype(F32) * mixed).astype(o_ref.dtype)


def _prompt_sgu(u, vn, sgu_w, bias_exp):
    G, R, C = u.shape
    tt = SGU_TT
    return pl.pallas_call(
        functools.partial(_sgu_body, tt=tt),
        grid=(G, R // tt),
        in_specs=[
            pl.BlockSpec((1, tt, C), lambda b, i: (b, i, 0)),
            pl.BlockSpec((1, tt, C), lambda b, i: (b, i, 0)),
            pl.BlockSpec((C_HEADS, CHUNK, CHUNK), lambda b, i: (0, 0, 0)),
            pl.BlockSpec((CHUNK, C), lambda b, i: (0, 0)),
        ],
        out_specs=pl.BlockSpec((1, tt, C), lambda b, i: (b, i, 0)),
        out_shape=jax.ShapeDtypeStruct((G, R, C), BF16),
        compiler_params=_cparams(("arbitrary", "arbitrary")),
        name="prompt_sgu",
    )(u, vn, sgu_w, bias_exp)


def _out_proj_body(oa_ref, ob_ref, oc_ref, w_ref, x_ref, gate_ref, o_ref, wbf_ref):
    @pl.when(_first_row_tile(1))
    def _():
        wbf_ref[...] = w_ref[...].astype(BF16)

    acc = jnp.dot(oa_ref[0], wbf_ref[0:A_WIDTH, :], preferred_element_type=F32)
    acc = acc + jnp.dot(ob_ref[0], wbf_ref[A_WIDTH:A_WIDTH + B_WIDTH, :], preferred_element_type=F32)
    acc = acc + jnp.dot(oc_ref[0], wbf_ref[A_WIDTH + B_WIDTH:, :], preferred_element_type=F32)
    o_ref[0] = x_ref[0] + gate_ref[0] * acc


def _out_proj(oa, ob, oc, w_out, layer, x, mod, gate_chunk, tm):
    G, R, D = x.shape
    rm = mod.shape[1]
    tn = TN_WIDE
    gb = gate_chunk * (D // tn)
    act = lambda width: pl.BlockSpec((1, tm, width), lambda j, g_, i: (g_, i, 0))
    return pl.pallas_call(
        _out_proj_body,
        grid=(D // tn, G, R // tm),
        in_specs=[
            act(A_WIDTH), act(B_WIDTH), act(C_WIDTH),
            pl.BlockSpec((None, D, tn), lambda j, g_, i: (layer, 0, j)),
            pl.BlockSpec((1, tm, tn), lambda j, g_, i: (g_, i, j)),
            pl.BlockSpec((1, rm, tn), lambda j, g_, i: (g_, 0, gb + j)),
        ],
        out_specs=pl.BlockSpec((1, tm, tn), lambda j, g_, i: (g_, i, j)),
        out_shape=jax.ShapeDtypeStruct((G, R, D), F32),
        scratch_shapes=[pltpu.VMEM((D, tn), BF16)],
        compiler_params=_cparams(("arbitrary", "arbitrary", "arbitrary")),
        name="out_proj",
    )(oa, ob, oc, w_out, x, mod)


def _gate_up_body(h_ref, wg_ref, wu_ref, o_ref, wg_bf, wu_bf):
    @pl.when(_first_row_tile(1))
    def _():
        wg_bf[...] = wg_ref[...].astype(BF16)
        wu_bf[...] = wu_ref[...].astype(BF16)

    h = h_ref[0]
    a = jnp.dot(h, wg_bf[...], preferred_element_type=F32)
    b = jnp.dot(h, wu_bf[...], preferred_element_type=F32)
    o_ref[0] = (_silu(a) * b).astype(o_ref.dtype)


def _gate_up(h, w_gate, w_up, layer, tm):
    G, R, K = h.shape
    tn = TN
    wspec = pl.BlockSpec((None, K, tn), lambda j, g_, i: (layer, 0, j))
    return pl.pallas_call(
        _gate_up_body,
        grid=(D_FF // tn, G, R // tm),
        in_specs=[pl.BlockSpec((1, tm, K), lambda j, g_, i: (g_, i, 0)), wspec, wspec],
        out_specs=pl.BlockSpec((1, tm, tn), lambda j, g_, i: (g_, i, j)),
        out_shape=jax.ShapeDtypeStruct((G, R, D_FF), BF16),
        scratch_shapes=[pltpu.VMEM((K, tn), BF16), pltpu.VMEM((K, tn), BF16)],
        compiler_params=_cparams(("arbitrary", "arbitrary", "arbitrary")),
        name="ffn_gate_up",
    )(h, w_gate, w_up)


def _down_body(h_ref, w_ref, x_ref, gate_ref, o_ref, wbf_ref):
    @pl.when(_first_row_tile(1))
    def _():
        wbf_ref[...] = w_ref[...].astype(BF16)

    acc = jnp.dot(h_ref[0], wbf_ref[...], preferred_element_type=F32)
    o_ref[0] = x_ref[0] + gate_ref[0] * acc


def _down(hid, w_down, layer, x, mod, gate_chunk, tm):
    G, R, D = x.shape
    K = hid.shape[2]
    rm = mod.shape[1]
    tn = TN
    gb = gate_chunk * (D // tn)
    return pl.pallas_call(
        _down_body,
        grid=(D // tn, G, R // tm),
        in_specs=[
            pl.BlockSpec((1, tm, K), lambda j, g_, i: (g_, i, 0)),
            pl.BlockSpec((None, K, tn), lambda j, g_, i: (layer, 0, j)),
            pl.BlockSpec((1, tm, tn), lambda j, g_, i: (g_, i, j)),
            pl.BlockSpec((1, rm, tn), lambda j, g_, i: (g_, 0, gb + j)),
        ],
        out_specs=pl.BlockSpec((1, tm, tn), lambda j, g_, i: (g_, i, j)),
        out_shape=jax.ShapeDtypeStruct((G, R, D), F32),
        scratch_shapes=[pltpu.VMEM((K, tn), BF16)],
        compiler_params=_cparams(("arbitrary", "arbitrary", "arbitrary")),
        name="ffn_down",
    )(hid, w_down, x, mod)


def _paged_attn_body(pt_ref, q_ref, bias_ref, u_ref, *refs, n_steps):
    del pt_ref
    P = PAGES_PER_STEP
    k_refs = refs[:P]
    v_refs = refs[P:2 * P]
    o_ref = refs[2 * P]
    acc_ref, run_ref = refs[2 * P + 1:]
    t = pl.program_id(1)
    rows = PAGE_SIZE * A_HEADS
    n_blk = rows // LANES

    @pl.when(t == 0)
    def _():
        acc_ref[...] = jnp.zeros_like(acc_ref)
        run_ref[...] = jnp.zeros_like(run_ref)

    scale = HEAD_DIM ** -0.5
    q = q_ref[0]
    z = jnp.concatenate(
        [lax.dot_general(q, k_refs[p][...].astype(BF16), _NT, preferred_element_type=F32) for p in range(P)],
        axis=0) * scale + bias_ref[...]
    lane = lax.broadcasted_iota(jnp.int32, (P * A_HEADS, rows), 1)
    head = lax.broadcasted_iota(jnp.int32, (P * A_HEADS, rows), 0)
    own = (lane % A_HEADS) == (head % A_HEADS)
    a, lb = _stick_logs(z)
    a = jnp.where(own, a, 0.0)
    stacked = jnp.concatenate([a[:, c * LANES:(c + 1) * LANES] for c in range(n_blk)], axis=0)
    r = jnp.dot(stacked.astype(BF16), u_ref[...], preferred_element_type=F32)
    seen = run_ref[...]
    later = []
    for p in range(P):
        blocks = [None] * n_blk
        for c in reversed(range(n_blk)):
            r0 = (c * P + p) * A_HEADS
            blocks[c] = r[r0:r0 + A_HEADS, :LANES] + seen
            seen = seen + r[r0:r0 + A_HEADS, LANES:]
        later.append(jnp.concatenate(blocks, axis=-1))
    run_ref[...] = seen
    w = jnp.where(own, jnp.exp(lb - jnp.concatenate(later, axis=0)), 0.0)
    acc = acc_ref[...]
    for p in range(P):
        wp = w[p * A_HEADS:(p + 1) * A_HEADS].astype(BF16)
        acc = acc + jnp.dot(wp, v_refs[p][...].astype(BF16), preferred_element_type=F32)
    acc_ref[...] = acc

    @pl.when(t == n_steps - 1)
    def _():
        o_ref[0] = acc


def _paged_attention(q, cache_k, cache_v, page_table, sb_bias_l, layer):
    B, n_pages = page_table.shape
    P = PAGES_PER_STEP
    n_steps = n_pages // P
    rows = PAGE_SIZE * A_HEADS

    def page_spec(p):
        return pl.BlockSpec((None, None, rows, HEAD_DIM),
                            lambda b, t, pt: (layer, pt[b, n_pages - 1 - (t * P + p)], 0, 0))

    grid_spec = pltpu.PrefetchScalarGridSpec(
        num_scalar_prefetch=1,
        grid=(B, n_steps),
        in_specs=[
            pl.BlockSpec((1, A_HEADS, HEAD_DIM), lambda b, t, pt: (b, 0, 0)),
            pl.BlockSpec((P * A_HEADS, 1), lambda b, t, pt: (0, 0)),
            pl.BlockSpec((LANES, 2 * LANES), lambda b, t, pt: (0, 0)),
        ] + [page_spec(p) for p in range(P)] * 2,
        out_specs=pl.BlockSpec((1, A_HEADS, HEAD_DIM), lambda b, t, pt: (b, 0, 0)),
        scratch_shapes=[pltpu.VMEM((A_HEADS, HEAD_DIM), F32), pltpu.VMEM((A_HEADS, LANES), F32)],
    )
    return pl.pallas_call(
        functools.partial(_paged_attn_body, n_steps=n_steps),
        grid_spec=grid_spec,
        out_shape=jax.ShapeDtypeStruct((B, A_HEADS, HEAD_DIM), F32),
        compiler_params=_cparams(("arbitrary", "arbitrary")),
        name="paged_attn",
    )(page_table, q, jnp.tile(sb_bias_l, P).reshape(P * A_HEADS, 1), _suffix_matrix(),
      *([cache_k] * P), *([cache_v] * P))


def _sample_mix_body(xg_ref, st_ref, cw_ref, cb_ref, lg_ref, lb_ref, u_ref, vn_ref, w00_ref, b0_ref,
                     ob_ref, oc_ref):
    hist = CONV_WIDTH - 1
    xg = xg_ref[0]
    y = jnp.sum(st_ref[...] * cw_ref[0:hist, :][None], axis=1) + xg * cw_ref[hist:CONV_WIDTH, :] + cb_ref[...]
    mu = jnp.mean(y, axis=-1, keepdims=True)
    yc = y - mu
    yn = yc * lax.rsqrt(jnp.mean(yc * yc, axis=-1, keepdims=True) + EPS)
    ob_ref[0] = _silu(yn * lg_ref[...] + lb_ref[...]).astype(ob_ref.dtype)
    mixed = vn_ref[0].astype(BF16).astype(F32) * w00_ref[...].astype(BF16).astype(F32) + b0_ref[...]
    oc_ref[0] = (u_ref[0] * mixed).astype(oc_ref.dtype)


def _sample_mix(xg, state, conv_w, conv_b, ln_g, ln_b, u, vn, w00e, b0e):
    _, B, C = xg.shape
    row = lambda a: a.reshape(1, C)
    out = jax.ShapeDtypeStruct((1, B, C), BF16)
    return pl.pallas_call(
        _sample_mix_body,
        out_shape=(out, out),
        compiler_params=pltpu.CompilerParams(vmem_limit_bytes=VMEM_LIMIT),
        name="sample_mix",
    )(xg, state, conv_w, row(conv_b), row(ln_g), row(ln_b), u, vn, row(w00e), row(b0e))


def _mixer_inputs(h, w_in, layer, q_gain, k_gain, sgu_gain, tm, gate_dtype):
    tn = TN_WIDE
    q, = _proj(h, w_in, layer, [COL_Q], tn, "headnorm", [A_WIDTH], [BF16], tm, gain=q_gain, name="proj_q")
    k, = _proj(h, w_in, layer, [COL_K], tn, "headnorm", [A_WIDTH], [F32], tm, gain=k_gain, name="proj_k")
    v, = _proj(h, w_in, layer, [COL_V], tn, "plain", [A_WIDTH], [F32], tm, name="proj_v")
    xg, = _proj(h, w_in, layer, [COL_GLU_A, COL_GLU_G], B_WIDTH, "glu", [B_WIDTH], [F32], tm, name="proj_glu")
    u, vn = _proj(h, w_in, layer, [COL_U], tn, "split_norm", [C_WIDTH, C_WIDTH], [gate_dtype, gate_dtype], tm,
                  gain=sgu_gain, name="proj_u_vc")
    return q, k, v, xg, u, vn


def _channel_sublayer(x, mod, norm2_g, w_gate, w_up, w_down, layer, tm, tm_down):
    h2 = _norm_mod(x, norm2_g, mod, 3, 4, tm)
    hid = _gate_up(h2, w_gate, w_up, layer, tm)
    return _down(hid, w_down, layer, x, mod, 5, tm_down)


def kernel(x_prompt, x_sample, cache_k, cache_v, state_conv, page_table, c_prompt, c_sample, norm1_g, norm2_g, w_ada, b_ada, w_in, q_norm_g, k_norm_g, sb_bias, conv_w, conv_b, conv_ln_g, conv_ln_b, sgu_norm_g, sgu_w, sgu_b, w_out, w_gate, w_up, w_down):
    b_p, seq, _ = x_prompt.shape
    b_s = x_sample.shape[0]
    n_pool = cache_k.shape[1]
    tm_p = TM_PROMPT
    tm_s = b_s

    n_c = b_p + b_s
    pad = (-n_c) % SUBLANES
    c_all = jnp.concatenate([c_prompt, c_sample, jnp.zeros((pad, D_MODEL), F32)], axis=0)
    mod = _ada_mod(c_all, w_ada, b_ada)
    mod_p = mod[:, :b_p].reshape(DEPTH, b_p, 1, N_MOD * D_MODEL)
    mod_s = mod[:, b_p:n_c].reshape(DEPTH, 1, b_s, N_MOD * D_MODEL)

    ck = cache_k.reshape(DEPTH, n_pool, PAGE_SIZE * A_HEADS, HEAD_DIM)
    cv = cache_v.reshape(DEPTH, n_pool, PAGE_SIZE * A_HEADS, HEAD_DIM)

    y_p = x_prompt
    y_s = x_sample.reshape(1, b_s, D_MODEL)
    kp_rows, vp_rows, ks_rows, vs_rows, conv_p, conv_s, sgu_s = [], [], [], [], [], [], []
    for l in range(DEPTH):
        q_gain = jnp.tile(q_norm_g[l], A_HEADS)
        k_gain = jnp.tile(k_norm_g[l], A_HEADS)
        sgu_bias = jnp.repeat(sgu_b[l].T, HEAD_DIM, axis=1)

        h = _norm_mod(y_p, norm1_g[l], mod_p[l], 0, 1, tm_p)
        q, k, v, xg, u, vn = _mixer_inputs(h, w_in, l, q_gain, k_gain, sgu_norm_g[l], tm_p, BF16)
        o_a = _prompt_attention(q, k, v, sb_bias, l)
        o_b = _prompt_conv(xg, conv_w[l], conv_b[l], conv_ln_g[l], conv_ln_b[l])
        o_c = _prompt_sgu(u, vn, sgu_w[l], sgu_bias)
        y_p = _out_proj(o_a, o_b, o_c, w_out, l, y_p, mod_p[l], 2, tm_p)
        y_p = _channel_sublayer(y_p, mod_p[l], norm2_g[l], w_gate, w_up, w_down, l, tm_p, TM_DOWN)
        kp_rows.append(k.reshape(b_p, seq, A_HEADS, HEAD_DIM))
        vp_rows.append(v.reshape(b_p, seq, A_HEADS, HEAD_DIM))
        conv_p.append(xg[:, seq - (CONV_WIDTH - 1):, :])

        h = _norm_mod(y_s, norm1_g[l], mod_s[l], 0, 1, tm_s)
        q, k, v, xg, u, vn = _mixer_inputs(h, w_in, l, q_gain, k_gain, sgu_norm_g[l], tm_s, F32)
        o_a = _paged_attention(q.reshape(b_s, A_HEADS, HEAD_DIM), ck, cv, page_table, sb_bias[l], l)
        o_a = o_a.reshape(1, b_s, A_WIDTH).astype(BF16)
        w00e = jnp.repeat(sgu_w[l][:, 0, 0], HEAD_DIM)
        b0e = jnp.repeat(sgu_b[l][:, 0], HEAD_DIM)
        o_b, o_c = _sample_mix(xg, state_conv[l], conv_w[l], conv_b[l], conv_ln_g[l], conv_ln_b[l],
                               u, vn, w00e, b0e)
        y_s = _out_proj(o_a, o_b, o_c, w_out, l, y_s, mod_s[l], 2, tm_s)
        y_s = _channel_sublayer(y_s, mod_s[l], norm2_g[l], w_gate, w_up, w_down, l, tm_s, tm_s)
        ks_rows.append(k.reshape(b_s, 1, A_HEADS, HEAD_DIM))
        vs_rows.append(v.reshape(b_s, 1, A_HEADS, HEAD_DIM))
        conv_s.append(jnp.concatenate([state_conv[l][:, 1:, :], xg.reshape(b_s, 1, B_WIDTH)], axis=1))
        sgu_s.append(vn.reshape(b_s, 1, C_WIDTH))

    return (y_p, y_s.reshape(b_s, 1, D_MODEL), jnp.stack(kp_rows), jnp.stack(vp_rows),
            jnp.stack(ks_rows), jnp.stack(vs_rows), jnp.stack(conv_p), jnp.stack(conv_s), jnp.stack(sgu_s))
```

```python
import functools

import jax
import jax.numpy as jnp
from jax import lax
from jax.experimental import pallas as pl
from jax.experimental.pallas import tpu as pltpu

F32 = jnp.float32
BF16 = jnp.bfloat16

D_MODEL = 2048
DEPTH = 4
PAGE_SIZE = 128
HEAD_DIM = 128
A_WIDTH = D_MODEL // 2
A_HEADS = A_WIDTH // HEAD_DIM
B_WIDTH = D_MODEL // 4
C_WIDTH = D_MODEL - A_WIDTH - B_WIDTH
C_HEADS = C_WIDTH // HEAD_DIM
CONV_WIDTH = 31
CHUNK = 128
D_FF = ((8 * D_MODEL // 3 + 255) // 256) * 256
N_MOD = 6
EPS = 1e-6

COL_Q = 0
COL_K = A_WIDTH
COL_V = 2 * A_WIDTH
COL_GLU_A = 3 * A_WIDTH
COL_GLU_G = COL_GLU_A + B_WIDTH
COL_U = COL_GLU_G + B_WIDTH
COL_VC = COL_U + C_WIDTH

LANES = 128
SUBLANES = 8
VMEM_LIMIT = 56 * 1024 * 1024
TM_PROMPT = 1024
TM_DOWN = 512
TN = 512
TN_WIDE = 1024
ATT_BLK = 256
ATT_HEADS = 8
ATT_GROUP = 4
CONV_TT = 256
CONV_HALO = 32
SGU_TT = 512
PAGES_PER_STEP = 16

_NT = (((1,), (1,)), ((), ()))
LOG2E = 1.4426950408889634


def _cparams(sem):
    return pltpu.CompilerParams(dimension_semantics=sem, vmem_limit_bytes=VMEM_LIMIT)


def _stick_logs2(z2):
    a2 = jnp.maximum(z2, 0.0) + jnp.log(1.0 + jnp.exp2(-jnp.abs(z2))) * LOG2E
    return a2, z2 - a2


def _silu(x):
    return x * jax.nn.sigmoid(x)


def _suffix_matrix(size):
    j = jnp.arange(size)[:, None]
    n = jnp.arange(size + LANES)[None, :]
    return jnp.logical_or(n >= size, j > n).astype(BF16)


def _ada_body(c_ref, w_ref, b_ref, o_ref):
    s = _silu(c_ref[...]).astype(BF16)
    o_ref[0] = jnp.dot(s, w_ref[...].astype(BF16), preferred_element_type=F32) + b_ref[0]


def _ada_mod(c_all, w_ada, b_ada):
    rows = c_all.shape[0]
    tn = 1024
    n_mod = w_ada.shape[2]
    return pl.pallas_call(
        _ada_body,
        grid=(DEPTH, n_mod // tn),
        in_specs=[
            pl.BlockSpec((rows, D_MODEL), lambda l, j: (0, 0)),
            pl.BlockSpec((None, D_MODEL, tn), lambda l, j: (l, 0, j)),
            pl.BlockSpec((1, 1, tn), lambda l, j: (l, 0, j)),
        ],
        out_specs=pl.BlockSpec((1, rows, tn), lambda l, j: (l, 0, j)),
        out_shape=jax.ShapeDtypeStruct((DEPTH, rows, n_mod), F32),
        compiler_params=_cparams(("arbitrary", "arbitrary")),
        name="ada_mod",
    )(c_all, w_ada, b_ada.reshape(DEPTH, 1, n_mod))


def _norm_mod_body(x_ref, g_ref, shift_ref, scale_ref, o_ref):
    x = x_ref[0]
    y = x * lax.rsqrt(jnp.mean(x * x, axis=-1, keepdims=True) + EPS)
    y = y * g_ref[...]
    o_ref[0] = (y * (1.0 + scale_ref[0]) + shift_ref[0]).astype(o_ref.dtype)


def _norm_mod(x, g, mod, shift_chunk, scale_chunk, tm):
    G, R, D = x.shape
    rm = mod.shape[1]
    return pl.pallas_call(
        _norm_mod_body,
        grid=(G, R // tm),
        in_specs=[
            pl.BlockSpec((1, tm, D), lambda g_, i: (g_, i, 0)),
            pl.BlockSpec((1, D), lambda g_, i: (0, 0)),
            pl.BlockSpec((1, rm, D), lambda g_, i: (g_, 0, shift_chunk)),
            pl.BlockSpec((1, rm, D), lambda g_, i: (g_, 0, scale_chunk)),
        ],
        out_specs=pl.BlockSpec((1, tm, D), lambda g_, i: (g_, i, 0)),
        out_shape=jax.ShapeDtypeStruct((G, R, D), BF16),
        compiler_params=_cparams(("arbitrary", "arbitrary")),
        name="norm_mod",
    )(x, g.reshape(1, D), mod, mod)


def _first_row_tile(g_axis):
    return jnp.logical_and(pl.program_id(g_axis) == 0, pl.program_id(g_axis + 1) == 0)


def _proj_epilogue(kind, accs, gain_ref):
    if kind == "plain":
        return [accs[0]]
    if kind == "glu":
        return [accs[0] * jax.nn.sigmoid(accs[1])]
    if kind == "headnorm":
        parts = []
        for hh in range(accs[0].shape[1] // HEAD_DIM):
            a = accs[0][:, hh * HEAD_DIM:(hh + 1) * HEAD_DIM]
            parts.append(a * lax.rsqrt(jnp.mean(a * a, axis=-1, keepdims=True) + EPS))
        return [jnp.concatenate(parts, axis=-1) * gain_ref[...]]
    half = accs[0].shape[1] // 2
    a = accs[0][:, half:]
    return [accs[0][:, :half], a * lax.rsqrt(jnp.mean(a * a, axis=-1, keepdims=True) + EPS) * gain_ref[...]]


def _proj_body(*refs, kind, n_w, n_out):
    h_ref, hs_ref = refs[:2]
    w_refs = refs[2:2 + n_w]
    has_gain = kind in ("headnorm", "split_norm")
    gain_ref = refs[2 + n_w] if has_gain else None
    first_out = 2 + n_w + int(has_gain)
    o_refs = refs[first_out:first_out + n_out]
    os_refs = refs[first_out + n_out:first_out + 2 * n_out]
    wbf_refs = refs[first_out + 2 * n_out:]

    def project(x, out_refs):
        accs = [jnp.dot(x, wbf[...], preferred_element_type=F32) for wbf in wbf_refs]
        for o_ref, out in zip(out_refs, _proj_epilogue(kind, accs, gain_ref)):
            o_ref[0] = out.astype(o_ref.dtype)

    @pl.when(_first_row_tile(0))
    def _():
        for w_ref, wbf in zip(w_refs, wbf_refs):
            wbf[...] = w_ref[...].astype(BF16)
        project(hs_ref[0], os_refs)

    project(h_ref[0], o_refs)


def _proj(h, h_s, w, layer, col_starts, tn, kind, out_widths, out_dtypes, out_dtypes_s, tm, gain=None, name="proj"):
    G, R, K = h.shape
    rs = h_s.shape[1]
    n_w = len(col_starts)
    in_specs = [pl.BlockSpec((1, tm, K), lambda g_, i: (g_, i, 0)), pl.BlockSpec((1, rs, K), lambda g_, i: (0, 0, 0))]
    args = [h, h_s]
    for cs in col_starts:
        in_specs.append(pl.BlockSpec((None, K, tn), lambda g_, i, cb=cs // tn: (layer, 0, cb)))
        args.append(w)
    if gain is not None:
        in_specs.append(pl.BlockSpec((1, gain.size), lambda g_, i: (0, 0)))
        args.append(gain.reshape(1, gain.size))
    return pl.pallas_call(
        functools.partial(_proj_body, kind=kind, n_w=n_w, n_out=len(out_widths)),
        grid=(G, R // tm),
        in_specs=in_specs,
        out_specs=([pl.BlockSpec((1, tm, wd), lambda g_, i: (g_, i, 0)) for wd in out_widths]
                   + [pl.BlockSpec((1, rs, wd), lambda g_, i: (0, 0, 0)) for wd in out_widths]),
        out_shape=([jax.ShapeDtypeStruct((G, R, wd), dt) for wd, dt in zip(out_widths, out_dtypes)]
                   + [jax.ShapeDtypeStruct((1, rs, wd), dt) for wd, dt in zip(out_widths, out_dtypes_s)]),
        scratch_shapes=[pltpu.VMEM((K, tn), BF16) for _ in range(n_w)],
        compiler_params=_cparams(("arbitrary", "arbitrary")),
        name=name,
    )(*args)


def _attn_body(bias_ref, q_ref, k_ref, v_ref, u_ref, o_ref, acc_ref, run_ref, *, layer, blk):
    hp = pl.program_id(1)
    qi = pl.program_id(2)
    scale2 = HEAD_DIM ** -0.5 * LOG2E
    row = lax.broadcasted_iota(jnp.int32, (blk, blk), 0)
    col = lax.broadcasted_iota(jnp.int32, (blk, blk), 1)
    acc_ref[...] = jnp.zeros_like(acc_ref)
    run_ref[...] = jnp.zeros_like(run_ref)

    def kv_block(j, masked):
        ks = pl.multiple_of(j * blk, blk)
        for e0 in range(0, ATT_HEADS, ATT_GROUP):
            head_group(ks, range(e0, e0 + ATT_GROUP), masked)

    def head_group(ks, heads, masked):
        a_bf, lbs = {}, {}
        for e in heads:
            cols = slice(e * HEAD_DIM, (e + 1) * HEAD_DIM)
            bias2 = bias_ref[layer, hp * ATT_HEADS + e] * LOG2E
            kb = k_ref[0, pl.ds(ks, blk), cols].astype(BF16)
            z2 = lax.dot_general(q_ref[0, :, cols], kb, _NT, preferred_element_type=F32) * scale2 + bias2
            a, lbs[e] = _stick_logs2(z2)
            if masked:
                a = jnp.where(col < row, a, 0.0)
            a_bf[e] = a.astype(BF16)
        rs = {e: jnp.dot(a_bf[e], u_ref[...], preferred_element_type=F32) for e in heads}
        for e in heads:
            cols = slice(e * HEAD_DIM, (e + 1) * HEAD_DIM)
            seen = run_ref[e]
            later = rs[e][:, :blk] + jnp.concatenate([seen] * (blk // LANES), axis=-1)
            w = jnp.exp2(lbs[e] - later)
            if masked:
                w = jnp.where(col < row, w, 0.0)
            run_ref[e] = seen + rs[e][:, blk:]
            vb = v_ref[0, pl.ds(ks, blk), cols].astype(BF16)
            acc_ref[e] += jnp.dot(w.astype(BF16), vb, preferred_element_type=F32)

    kv_block(qi, True)

    def body(it, carry):
        kv_block(qi - 1 - it, False)
        return carry

    lax.fori_loop(0, qi, body, 0)
    for e in range(ATT_HEADS):
        o_ref[0, :, e * HEAD_DIM:(e + 1) * HEAD_DIM] = acc_ref[e].astype(o_ref.dtype)


def _prompt_attention(q, k, v, sb_bias, layer):
    G, R, _ = q.shape
    blk = ATT_BLK
    wd = ATT_HEADS * HEAD_DIM
    return pl.pallas_call(
        functools.partial(_attn_body, layer=layer, blk=blk),
        grid=(G, A_HEADS // ATT_HEADS, R // blk),
        in_specs=[
            pl.BlockSpec(memory_space=pltpu.SMEM),
            pl.BlockSpec((1, blk, wd), lambda b, h, i: (b, i, h)),
            pl.BlockSpec((1, R, wd), lambda b, h, i: (b, 0, h)),
            pl.BlockSpec((1, R, wd), lambda b, h, i: (b, 0, h)),
            pl.BlockSpec((blk, blk + LANES), lambda b, h, i: (0, 0)),
        ],
        out_specs=pl.BlockSpec((1, blk, wd), lambda b, h, i: (b, i, h)),
        out_shape=jax.ShapeDtypeStruct((G, R, A_WIDTH), BF16),
        scratch_shapes=[pltpu.VMEM((ATT_HEADS, blk, HEAD_DIM), F32), pltpu.VMEM((ATT_HEADS, blk, LANES), F32)],
        compiler_params=_cparams(("arbitrary", "arbitrary", "arbitrary")),
        name="prompt_attn",
    )(sb_bias, q, k, v, _suffix_matrix(blk))


def _conv_body(cur_ref, halo_ref, cw_ref, cb_ref, lg_ref, lb_ref, o_ref, xp_ref, xs_ref, *, tt):
    i = pl.program_id(1)
    xp_ref[0:CONV_HALO, :] = jnp.where(i > 0, halo_ref[0], 0.0)
    xp_ref[CONV_HALO:, :] = cur_ref[0]
    span = tt + CONV_HALO - SUBLANES
    for r in range(1, SUBLANES):
        xs_ref[r - 1] = xp_ref[pl.ds(r, span), :]
    first = CONV_HALO - (CONV_WIDTH - 1)
    y = cb_ref[...]
    for w in range(CONV_WIDTH):
        r = (first + w) % SUBLANES
        base = first + w - r
        win = xp_ref[pl.ds(base, tt), :] if r == 0 else xs_ref[r - 1, pl.ds(base, tt), :]
        y = y + cw_ref[w:w + 1, :] * win
    mu = jnp.mean(y, axis=-1, keepdims=True)
    yc = y - mu
    yn = yc * lax.rsqrt(jnp.mean(yc * yc, axis=-1, keepdims=True) + EPS)
    o_ref[0] = _silu(yn * lg_ref[...] + lb_ref[...]).astype(o_ref.dtype)


def _prompt_conv(xg, conv_w, conv_b, ln_g, ln_b):
    G, R, C = xg.shape
    tt = CONV_TT
    hb = tt // CONV_HALO
    row = lambda a: a.reshape(1, C)
    return pl.pallas_call(
        functools.partial(_conv_body, tt=tt),
        grid=(G, R // tt),
        in_specs=[
            pl.BlockSpec((1, tt, C), lambda b, i: (b, i, 0)),
            pl.BlockSpec((1, CONV_HALO, C), lambda b, i: (b, jnp.maximum(i * hb - 1, 0), 0)),
            pl.BlockSpec((CONV_WIDTH, C), lambda b, i: (0, 0)),
            pl.BlockSpec((1, C), lambda b, i: (0, 0)),
            pl.BlockSpec((1, C), lambda b, i: (0, 0)),
            pl.BlockSpec((1, C), lambda b, i: (0, 0)),
        ],
        out_specs=pl.BlockSpec((1, tt, C), lambda b, i: (b, i, 0)),
        out_shape=jax.ShapeDtypeStruct((G, R, C), BF16),
        scratch_shapes=[pltpu.VMEM((tt + CONV_HALO, C), F32),
                        pltpu.VMEM((SUBLANES - 1, tt + CONV_HALO - SUBLANES, C), F32)],
        compiler_params=_cparams(("arbitrary", "arbitrary")),
        name="prompt_conv",
    )(xg, xg, conv_w, row(conv_b), row(ln_g), row(ln_b))


def _sgu_body(u_ref, v_ref, w_ref, b_ref, o_ref, *, tt):
    t = lax.broadcasted_iota(jnp.int32, (CHUNK, CHUNK), 0)
    s = lax.broadcasted_iota(jnp.int32, (CHUNK, CHUNK), 1)
    for hh in range(C_HEADS):
        wc = jnp.where(s <= t, w_ref[hh], 0.0).astype(BF16)
        cols = slice(hh * HEAD_DIM, (hh + 1) * HEAD_DIM)
        for c in range(tt // CHUNK):
            rows = slice(c * CHUNK, (c + 1) * CHUNK)
            mixed = jnp.dot(wc, v_ref[0, rows, cols].astype(BF16), preferred_element_type=F32) + b_ref[:, cols]
            o_ref[0, rows, cols] = (u_ref[0, rows, cols].astype(F32) * mixed).astype(o_ref.dtype)


def _prompt_sgu(u, vn, sgu_w, bias_exp):
    G, R, C = u.shape
    tt = SGU_TT
    return pl.pallas_call(
        functools.partial(_sgu_body, tt=tt),
        grid=(G, R // tt),
        in_specs=[
            pl.BlockSpec((1, tt, C), lambda b, i: (b, i, 0)),
            pl.BlockSpec((1, tt, C), lambda b, i: (b, i, 0)),
            pl.BlockSpec((C_HEADS, CHUNK, CHUNK), lambda b, i: (0, 0, 0)),
            pl.BlockSpec((CHUNK, C), lambda b, i: (0, 0)),
        ],
        out_specs=pl.BlockSpec((1, tt, C), lambda b, i: (b, i, 0)),
        out_shape=jax.ShapeDtypeStruct((G, R, C), BF16),
        compiler_params=_cparams(("arbitrary", "arbitrary")),
        name="prompt_sgu",
    )(u, vn, sgu_w, bias_exp)


def _out_proj_body(oa_ref, ob_ref, oc_ref, oas_ref, obs_ref, ocs_ref, w_ref, x_ref, gate_ref, xs_ref, gates_ref,
                   o_ref, os_ref, wbf_ref):
    def project(oa, ob, oc, x, gate, out_ref):
        acc = jnp.dot(oa[0], wbf_ref[0:A_WIDTH, :], preferred_element_type=F32)
        acc = acc + jnp.dot(ob[0], wbf_ref[A_WIDTH:A_WIDTH + B_WIDTH, :], preferred_element_type=F32)
        acc = acc + jnp.dot(oc[0], wbf_ref[A_WIDTH + B_WIDTH:, :], preferred_element_type=F32)
        out_ref[0] = x[0] + gate[0] * acc

    @pl.when(_first_row_tile(1))
    def _():
        wbf_ref[...] = w_ref[...].astype(BF16)
        project(oas_ref, obs_ref, ocs_ref, xs_ref, gates_ref, os_ref)

    project(oa_ref, ob_ref, oc_ref, x_ref, gate_ref, o_ref)


def _row_specs(tm, width):
    return pl.BlockSpec((1, tm, width), lambda j, g_, i: (g_, i, 0))


def _sample_specs(rs, width):
    return pl.BlockSpec((1, rs, width), lambda j, g_, i: (0, 0, 0))


def _residual_specs(tm, rm, rs, tn, gb):
    return [
        pl.BlockSpec((1, tm, tn), lambda j, g_, i: (g_, i, j)),
        pl.BlockSpec((1, rm, tn), lambda j, g_, i: (g_, 0, gb + j)),
        pl.BlockSpec((1, rs, tn), lambda j, g_, i: (0, 0, j)),
        pl.BlockSpec((1, rs, tn), lambda j, g_, i: (0, 0, gb + j)),
    ]


def _residual_outs(G, R, rs, D, tm, tn):
    specs = [pl.BlockSpec((1, tm, tn), lambda j, g_, i: (g_, i, j)),
             pl.BlockSpec((1, rs, tn), lambda j, g_, i: (0, 0, j))]
    shapes = [jax.ShapeDtypeStruct((G, R, D), F32), jax.ShapeDtypeStruct((1, rs, D), F32)]
    return specs, shapes


def _out_proj(o_p, o_s, w_out, layer, x, x_s, mod, mod_s, gate_chunk, tm):
    G, R, D = x.shape
    rm, rs = mod.shape[1], x_s.shape[1]
    tn = TN_WIDE
    gb = gate_chunk * (D // tn)
    widths = (A_WIDTH, B_WIDTH, C_WIDTH)
    out_specs, out_shape = _residual_outs(G, R, rs, D, tm, tn)
    return pl.pallas_call(
        _out_proj_body,
        grid=(D // tn, G, R // tm),
        in_specs=([_row_specs(tm, wd) for wd in widths] + [_sample_specs(rs, wd) for wd in widths]
                  + [pl.BlockSpec((None, D, tn), lambda j, g_, i: (layer, 0, j))]
                  + _residual_specs(tm, rm, rs, tn, gb)),
        out_specs=out_specs,
        out_shape=out_shape,
        scratch_shapes=[pltpu.VMEM((D, tn), BF16)],
        compiler_params=_cparams(("arbitrary", "arbitrary", "arbitrary")),
        name="out_proj",
    )(*o_p, *o_s, w_out, x, mod, x_s, mod_s)


def _gate_up_body(h_ref, hs_ref, wg_ref, wu_ref, o_ref, os_ref, wg_bf, wu_bf):
    def project(x, out_ref):
        a = jnp.dot(x[0], wg_bf[...], preferred_element_type=F32)
        b = jnp.dot(x[0], wu_bf[...], preferred_element_type=F32)
        out_ref[0] = (_silu(a) * b).astype(out_ref.dtype)

    @pl.when(_first_row_tile(1))
    def _():
        wg_bf[...] = wg_ref[...].astype(BF16)
        wu_bf[...] = wu_ref[...].astype(BF16)
        project(hs_ref, os_ref)

    project(h_ref, o_ref)


def _gate_up(h, h_s, w_gate, w_up, layer, tm):
    G, R, K = h.shape
    rs = h_s.shape[1]
    tn = TN
    wspec = pl.BlockSpec((None, K, tn), lambda j, g_, i: (layer, 0, j))
    return pl.pallas_call(
        _gate_up_body,
        grid=(D_FF // tn, G, R // tm),
        in_specs=[_row_specs(tm, K), _sample_specs(rs, K), wspec, wspec],
        out_specs=[pl.BlockSpec((1, tm, tn), lambda j, g_, i: (g_, i, j)),
                   pl.BlockSpec((1, rs, tn), lambda j, g_, i: (0, 0, j))],
        out_shape=[jax.ShapeDtypeStruct((G, R, D_FF), BF16), jax.ShapeDtypeStruct((1, rs, D_FF), BF16)],
        scratch_shapes=[pltpu.VMEM((K, tn), BF16), pltpu.VMEM((K, tn), BF16)],
        compiler_params=_cparams(("arbitrary", "arbitrary", "arbitrary")),
        name="ffn_gate_up",
    )(h, h_s, w_gate, w_up)


def _down_body(h_ref, hs_ref, w_ref, x_ref, gate_ref, xs_ref, gates_ref, o_ref, os_ref, wbf_ref):
    def project(hid, x, gate, out_ref):
        out_ref[0] = x[0] + gate[0] * jnp.dot(hid[0], wbf_ref[...], preferred_element_type=F32)

    @pl.when(_first_row_tile(1))
    def _():
        wbf_ref[...] = w_ref[...].astype(BF16)
        project(hs_ref, xs_ref, gates_ref, os_ref)

    project(h_ref, x_ref, gate_ref, o_ref)


def _down(hid, hid_s, w_down, layer, x, x_s, mod, mod_s, gate_chunk, tm):
    G, R, D = x.shape
    K = hid.shape[2]
    rm, rs = mod.shape[1], x_s.shape[1]
    tn = TN
    gb = gate_chunk * (D // tn)
    out_specs, out_shape = _residual_outs(G, R, rs, D, tm, tn)
    return pl.pallas_call(
        _down_body,
        grid=(D // tn, G, R // tm),
        in_specs=([_row_specs(tm, K), _sample_specs(rs, K),
                   pl.BlockSpec((None, K, tn), lambda j, g_, i: (layer, 0, j))]
                  + _residual_specs(tm, rm, rs, tn, gb)),
        out_specs=out_specs,
        out_shape=out_shape,
        scratch_shapes=[pltpu.VMEM((K, tn), BF16)],
        compiler_params=_cparams(("arbitrary", "arbitrary", "arbitrary")),
        name="ffn_down",
    )(hid, hid_s, w_down, x, mod, x_s, mod_s)


def _paged_attn_body(pt_ref, q_ref, bias_ref, u_ref, *refs, n_steps):
    del pt_ref
    P = PAGES_PER_STEP
    k_refs = refs[:P]
    v_refs = refs[P:2 * P]
    o_ref = refs[2 * P]
    acc_ref, run_ref = refs[2 * P + 1:]
    t = pl.program_id(1)
    rows = PAGE_SIZE * A_HEADS
    n_blk = rows // LANES

    @pl.when(t == 0)
    def _():
        acc_ref[...] = jnp.zeros_like(acc_ref)
        run_ref[...] = jnp.zeros_like(run_ref)

    scale2 = HEAD_DIM ** -0.5 * LOG2E
    q = q_ref[0]
    z2 = jnp.concatenate(
        [lax.dot_general(q, k_refs[p][...].astype(BF16), _NT, preferred_element_type=F32) for p in range(P)],
        axis=0) * scale2 + bias_ref[...] * LOG2E
    lane = lax.broadcasted_iota(jnp.int32, (P * A_HEADS, rows), 1)
    head = lax.broadcasted_iota(jnp.int32, (P * A_HEADS, rows), 0)
    own = (lane % A_HEADS) == (head % A_HEADS)
    a, lb = _stick_logs2(z2)
    a = jnp.where(own, a, 0.0)
    stacked = jnp.concatenate([a[:, c * LANES:(c + 1) * LANES] for c in range(n_blk)], axis=0)
    r = jnp.dot(stacked.astype(BF16), u_ref[...], preferred_element_type=F32)
    seen = run_ref[...]
    later = []
    for p in range(P):
        blocks = [None] * n_blk
        for c in reversed(range(n_blk)):
            r0 = (c * P + p) * A_HEADS
            blocks[c] = r[r0:r0 + A_HEADS, :LANES] + seen
            seen = seen + r[r0:r0 + A_HEADS, LANES:]
        later.append(jnp.concatenate(blocks, axis=-1))
    run_ref[...] = seen
    w = jnp.where(own, jnp.exp2(lb - jnp.concatenate(later, axis=0)), 0.0)
    acc = acc_ref[...]
    for p in range(P):
        wp = w[p * A_HEADS:(p + 1) * A_HEADS].astype(BF16)
        acc = acc + jnp.dot(wp, v_refs[p][...].astype(BF16), preferred_element_type=F32)
    acc_ref[...] = acc

    @pl.when(t == n_steps - 1)
    def _():
        o_ref[0] = acc


def _paged_attention(q, cache_k, cache_v, page_table, sb_bias_l, layer):
    B, n_pages = page_table.shape
    P = PAGES_PER_STEP
    n_steps = n_pages // P
    rows = PAGE_SIZE * A_HEADS

    def page_spec(p):
        return pl.BlockSpec((None, None, rows, HEAD_DIM),
                            lambda b, t, pt: (layer, pt[b, n_pages - 1 - (t * P + p)], 0, 0))

    grid_spec = pltpu.PrefetchScalarGridSpec(
        num_scalar_prefetch=1,
        grid=(B, n_steps),
        in_specs=[
            pl.BlockSpec((1, A_HEADS, HEAD_DIM), lambda b, t, pt: (b, 0, 0)),
            pl.BlockSpec((P * A_HEADS, 1), lambda b, t, pt: (0, 0)),
            pl.BlockSpec((LANES, 2 * LANES), lambda b, t, pt: (0, 0)),
        ] + [page_spec(p) for p in range(P)] * 2,
        out_specs=pl.BlockSpec((1, A_HEADS, HEAD_DIM), lambda b, t, pt: (b, 0, 0)),
        scratch_shapes=[pltpu.VMEM((A_HEADS, HEAD_DIM), F32), pltpu.VMEM((A_HEADS, LANES), F32)],
    )
    return pl.pallas_call(
        functools.partial(_paged_attn_body, n_steps=n_steps),
        grid_spec=grid_spec,
        out_shape=jax.ShapeDtypeStruct((B, A_HEADS, HEAD_DIM), F32),
        compiler_params=_cparams(("arbitrary", "arbitrary")),
        name="paged_attn",
    )(page_table, q, jnp.tile(sb_bias_l, P).reshape(P * A_HEADS, 1), _suffix_matrix(LANES),
      *([cache_k] * P), *([cache_v] * P))


def _sample_mix_body(xg_ref, st_ref, cw_ref, cb_ref, lg_ref, lb_ref, u_ref, vn_ref, w00_ref, b0_ref,
                     ob_ref, oc_ref):
    hist = CONV_WIDTH - 1
    xg = xg_ref[0]
    y = jnp.sum(st_ref[...] * cw_ref[0:hist, :][None], axis=1) + xg * cw_ref[hist:CONV_WIDTH, :] + cb_ref[...]
    mu = jnp.mean(y, axis=-1, keepdims=True)
    yc = y - mu
    yn = yc * lax.rsqrt(jnp.mean(yc * yc, axis=-1, keepdims=True) + EPS)
    ob_ref[0] = _silu(yn * lg_ref[...] + lb_ref[...]).astype(ob_ref.dtype)
    mixed = vn_ref[0].astype(BF16).astype(F32) * w00_ref[...].astype(BF16).astype(F32) + b0_ref[...]
    oc_ref[0] = (u_ref[0] * mixed).astype(oc_ref.dtype)


def _sample_mix(xg, state, conv_w, conv_b, ln_g, ln_b, u, vn, w00e, b0e):
    _, B, C = xg.shape
    row = lambda a: a.reshape(1, C)
    out = jax.ShapeDtypeStruct((1, B, C), BF16)
    return pl.pallas_call(
        _sample_mix_body,
        out_shape=(out, out),
        compiler_params=pltpu.CompilerParams(vmem_limit_bytes=VMEM_LIMIT),
        name="sample_mix",
    )(xg, state, conv_w, row(conv_b), row(ln_g), row(ln_b), u, vn, row(w00e), row(b0e))


def _mixer_inputs(h, h_s, w_in, layer, q_gain, k_gain, sgu_gain, tm):
    tn = TN_WIDE
    q, q_s = _proj(h, h_s, w_in, layer, [COL_Q], tn, "headnorm", [A_WIDTH], [BF16], [BF16], tm,
                   gain=q_gain, name="proj_q")
    k, k_s = _proj(h, h_s, w_in, layer, [COL_K], tn, "headnorm", [A_WIDTH], [F32], [F32], tm,
                   gain=k_gain, name="proj_k")
    v, v_s = _proj(h, h_s, w_in, layer, [COL_V], tn, "plain", [A_WIDTH], [F32], [F32], tm, name="proj_v")
    xg, xg_s = _proj(h, h_s, w_in, layer, [COL_GLU_A, COL_GLU_G], B_WIDTH, "glu", [B_WIDTH], [F32], [F32], tm,
                     name="proj_glu")
    u, vn, u_s, vn_s = _proj(h, h_s, w_in, layer, [COL_U], tn, "split_norm", [C_WIDTH, C_WIDTH], [BF16, BF16],
                             [F32, F32], tm, gain=sgu_gain, name="proj_u_vc")
    return (q, k, v, xg, u, vn), (q_s, k_s, v_s, xg_s, u_s, vn_s)


def kernel(x_prompt, x_sample, cache_k, cache_v, state_conv, page_table, c_prompt, c_sample, norm1_g, norm2_g, w_ada, b_ada, w_in, q_norm_g, k_norm_g, sb_bias, conv_w, conv_b, conv_ln_g, conv_ln_b, sgu_norm_g, sgu_w, sgu_b, w_out, w_gate, w_up, w_down):
    b_p, seq, _ = x_prompt.shape
    b_s = x_sample.shape[0]
    n_pool = cache_k.shape[1]
    tm_p = TM_PROMPT
    tm_s = b_s

    n_c = b_p + b_s
    pad = (-n_c) % SUBLANES
    c_all = jnp.concatenate([c_prompt, c_sample, jnp.zeros((pad, D_MODEL), F32)], axis=0)
    mod = _ada_mod(c_all, w_ada, b_ada)
    mod_p = mod[:, :b_p].reshape(DEPTH, b_p, 1, N_MOD * D_MODEL)
    mod_s = mod[:, b_p:n_c].reshape(DEPTH, 1, b_s, N_MOD * D_MODEL)

    ck = cache_k.reshape(DEPTH, n_pool, PAGE_SIZE * A_HEADS, HEAD_DIM)
    cv = cache_v.reshape(DEPTH, n_pool, PAGE_SIZE * A_HEADS, HEAD_DIM)

    y_p = x_prompt
    y_s = x_sample.reshape(1, b_s, D_MODEL)
    kp_rows, vp_rows, ks_rows, vs_rows, conv_p, conv_s, sgu_s = [], [], [], [], [], [], []
    for l in range(DEPTH):
        q_gain = jnp.tile(q_norm_g[l], A_HEADS)
        k_gain = jnp.tile(k_norm_g[l], A_HEADS)
        sgu_bias = jnp.repeat(sgu_b[l].T, HEAD_DIM, axis=1)

        h = _norm_mod(y_p, norm1_g[l], mod_p[l], 0, 1, tm_p)
        h_s = _norm_mod(y_s, norm1_g[l], mod_s[l], 0, 1, tm_s)
        (q, k, v, xg, u, vn), (q_s, k_s, v_s, xg_s, u_s, vn_s) = _mixer_inputs(
            h, h_s, w_in, l, q_gain, k_gain, sgu_norm_g[l], tm_p)
        o_a = _prompt_attention(q, k, v, sb_bias, l)
        o_b = _prompt_conv(xg, conv_w[l], conv_b[l], conv_ln_g[l], conv_ln_b[l])
        o_c = _prompt_sgu(u, vn, sgu_w[l], sgu_bias)
        o_as = _paged_attention(q_s.reshape(b_s, A_HEADS, HEAD_DIM), ck, cv, page_table, sb_bias[l], l)
        o_as = o_as.reshape(1, b_s, A_WIDTH).astype(BF16)
        w00e = jnp.repeat(sgu_w[l][:, 0, 0], HEAD_DIM)
        b0e = jnp.repeat(sgu_b[l][:, 0], HEAD_DIM)
        o_bs, o_cs = _sample_mix(xg_s, state_conv[l], conv_w[l], conv_b[l], conv_ln_g[l], conv_ln_b[l],
                                 u_s, vn_s, w00e, b0e)
        y_p, y_s = _out_proj((o_a, o_b, o_c), (o_as, o_bs, o_cs), w_out, l, y_p, y_s, mod_p[l], mod_s[l], 2, tm_p)

        h2 = _norm_mod(y_p, norm2_g[l], mod_p[l], 3, 4, tm_p)
        h2_s = _norm_mod(y_s, norm2_g[l], mod_s[l], 3, 4, tm_s)
        hid, hid_s = _gate_up(h2, h2_s, w_gate, w_up, l, tm_p)
        y_p, y_s = _down(hid, hid_s, w_down, l, y_p, y_s, mod_p[l], mod_s[l], 5, TM_DOWN)

        kp_rows.append(k.reshape(b_p, seq, A_HEADS, HEAD_DIM))
        vp_rows.append(v.reshape(b_p, seq, A_HEADS, HEAD_DIM))
        conv_p.append(xg[:, seq - (CONV_WIDTH - 1):, :])
        ks_rows.append(k_s.reshape(b_s, 1, A_HEADS, HEAD_DIM))
        vs_rows.append(v_s.reshape(b_s, 1, A_HEADS, HEAD_DIM))
        conv_s.append(jnp.concatenate([state_conv[l][:, 1:, :], xg_s.reshape(b_s, 1, B_WIDTH)], axis=1))
        sgu_s.append(vn_s.reshape(b_s, 1, C_WIDTH))

    return (y_p, y_s.reshape(b_s, 1, D_MODEL), jnp.stack(kp_rows), jnp.stack(vp_rows),
            jnp.stack(ks_rows), jnp.stack(vs_rows), jnp.stack(conv_p), jnp.stack(conv_s), jnp.stack(sgu_s))
```

```python
import functools

import jax
import jax.numpy as jnp
from jax import lax
from jax.experimental import pallas as pl
from jax.experimental.pallas import tpu as pltpu

F32 = jnp.float32
BF16 = jnp.bfloat16

D_MODEL = 2048
DEPTH = 4
PAGE_SIZE = 128
HEAD_DIM = 128
A_WIDTH = D_MODEL // 2
A_HEADS = A_WIDTH // HEAD_DIM
B_WIDTH = D_MODEL // 4
C_WIDTH = D_MODEL - A_WIDTH - B_WIDTH
C_HEADS = C_WIDTH // HEAD_DIM
CONV_WIDTH = 31
CHUNK = 128
D_FF = ((8 * D_MODEL // 3 + 255) // 256) * 256
N_MOD = 6
EPS = 1e-6

COL_Q = 0
COL_K = A_WIDTH
COL_V = 2 * A_WIDTH
COL_GLU_A = 3 * A_WIDTH
COL_GLU_G = COL_GLU_A + B_WIDTH
COL_U = COL_GLU_G + B_WIDTH
COL_VC = COL_U + C_WIDTH

LANES = 128
SUBLANES = 8
VMEM_LIMIT = 56 * 1024 * 1024
TM_PROMPT = 1024
TM_DOWN = 512
TM_IN = 256
TM_OUT = 512
W_CHUNK = 512
TN = 512
ATT_BLK = 256
ATT_HEADS = 8
ATT_GROUP = 4
CONV_TT = 256
CONV_HALO = 32
SGU_TT = 512
PAGES_PER_STEP = 16

_NT = (((1,), (1,)), ((), ()))
LOG2E = 1.4426950408889634


def _cparams(sem):
    return pltpu.CompilerParams(dimension_semantics=sem, vmem_limit_bytes=VMEM_LIMIT)


def _stick_logs2(z2):
    a2 = jnp.maximum(z2, 0.0) + jnp.log(1.0 + jnp.exp2(-jnp.abs(z2))) * LOG2E
    return a2, z2 - a2


def _silu(x):
    return x * jax.nn.sigmoid(x)


def _suffix_matrix(size):
    j = jnp.arange(size)[:, None]
    n = jnp.arange(size + LANES)[None, :]
    return jnp.logical_or(n >= size, j > n).astype(BF16)


def _ada_body(c_ref, w_ref, b_ref, o_ref):
    s = _silu(c_ref[...]).astype(BF16)
    o_ref[0] = jnp.dot(s, w_ref[...].astype(BF16), preferred_element_type=F32) + b_ref[0]


def _ada_mod(c_all, w_ada, b_ada):
    rows = c_all.shape[0]
    tn = 1024
    n_mod = w_ada.shape[2]
    return pl.pallas_call(
        _ada_body,
        grid=(DEPTH, n_mod // tn),
        in_specs=[
            pl.BlockSpec((rows, D_MODEL), lambda l, j: (0, 0)),
            pl.BlockSpec((None, D_MODEL, tn), lambda l, j: (l, 0, j)),
            pl.BlockSpec((1, 1, tn), lambda l, j: (l, 0, j)),
        ],
        out_specs=pl.BlockSpec((1, rows, tn), lambda l, j: (l, 0, j)),
        out_shape=jax.ShapeDtypeStruct((DEPTH, rows, n_mod), F32),
        compiler_params=_cparams(("arbitrary", "arbitrary")),
        name="ada_mod",
    )(c_all, w_ada, b_ada.reshape(DEPTH, 1, n_mod))


def _first_row_tile(g_axis):
    return jnp.logical_and(pl.program_id(g_axis) == 0, pl.program_id(g_axis + 1) == 0)


def _norm_rows(x, g, shift, scale):
    y = x * lax.rsqrt(jnp.mean(x * x, axis=-1, keepdims=True) + EPS)
    return ((y * g) * (1.0 + scale) + shift).astype(BF16)


def _rms(a):
    return a * lax.rsqrt(jnp.mean(a * a, axis=-1, keepdims=True) + EPS)


def _load_weights_bf16(w_hbm, layer, wbf_ref, stage_ref, sem):
    chunk = stage_ref.shape[2]
    n_chunks = wbf_ref.shape[1] // chunk

    def copy(c):
        slot = c % 2
        return pltpu.make_async_copy(w_hbm.at[layer, :, pl.ds(c * chunk, chunk)], stage_ref.at[slot], sem.at[slot])

    copy(0).start()
    for c in range(n_chunks):
        if c + 1 < n_chunks:
            copy(c + 1).start()
        copy(c).wait()
        wbf_ref[:, c * chunk:(c + 1) * chunk] = stage_ref[c % 2].astype(BF16)


def _resident_weight_scratch(K, N):
    return [pltpu.VMEM((K, N), BF16), pltpu.VMEM((2, K, W_CHUNK), F32), pltpu.SemaphoreType.DMA((2,))]


def _in_proj_body(x_ref, shift_ref, scale_ref, xs_ref, shifts_ref, scales_ref, g_ref, qg_ref, kg_ref, sg_ref, w_hbm,
                  *refs, layer):
    outs_p, outs_s = refs[:6], refs[6:12]
    wbf_ref, stage_ref, sem = refs[12:]

    def project(x, shift, scale, outs):
        h = _norm_rows(x, g_ref[...], shift, scale)
        seg = lambda c0, wd: jnp.dot(h, wbf_ref[:, c0:c0 + wd], preferred_element_type=F32)
        q_o, k_o, v_o, xg_o, u_o, vn_o = outs
        for o_ref, c0, gain_ref in ((q_o, COL_Q, qg_ref), (k_o, COL_K, kg_ref)):
            a = seg(c0, A_WIDTH)
            heads = [_rms(a[:, hh * HEAD_DIM:(hh + 1) * HEAD_DIM]) for hh in range(A_HEADS)]
            o_ref[0] = (jnp.concatenate(heads, axis=-1) * gain_ref[...]).astype(o_ref.dtype)
        v_o[0] = seg(COL_V, A_WIDTH).astype(v_o.dtype)
        xg_o[0] = (seg(COL_GLU_A, B_WIDTH) * jax.nn.sigmoid(seg(COL_GLU_G, B_WIDTH))).astype(xg_o.dtype)
        u_o[0] = seg(COL_U, C_WIDTH).astype(u_o.dtype)
        vn_o[0] = (_rms(seg(COL_VC, C_WIDTH)) * sg_ref[...]).astype(vn_o.dtype)

    @pl.when(_first_row_tile(0))
    def _():
        _load_weights_bf16(w_hbm, layer, wbf_ref, stage_ref, sem)
        project(xs_ref[0], shifts_ref[0], scales_ref[0], outs_s)

    project(x_ref[0], shift_ref[0], scale_ref[0], outs_p)


def _in_proj(x, x_s, mod, mod_s, norm_g, w_in, layer, q_gain, k_gain, sgu_gain):
    G, R, D = x.shape
    rm, rs = mod.shape[1], x_s.shape[1]
    tm = TM_IN
    n_in = w_in.shape[2]
    widths = (A_WIDTH, A_WIDTH, A_WIDTH, B_WIDTH, C_WIDTH, C_WIDTH)
    dtypes_p = (BF16, F32, F32, F32, BF16, BF16)
    dtypes_s = (BF16, F32, F32, F32, F32, F32)
    vec = lambda n: pl.BlockSpec((1, n), lambda g_, i: (0, 0))
    return pl.pallas_call(
        functools.partial(_in_proj_body, layer=layer),
        grid=(G, R // tm),
        in_specs=[
            pl.BlockSpec((1, tm, D), lambda g_, i: (g_, i, 0)),
            pl.BlockSpec((1, rm, D), lambda g_, i: (g_, 0, 0)),
            pl.BlockSpec((1, rm, D), lambda g_, i: (g_, 0, 1)),
            pl.BlockSpec((1, rs, D), lambda g_, i: (0, 0, 0)),
            pl.BlockSpec((1, rs, D), lambda g_, i: (0, 0, 0)),
            pl.BlockSpec((1, rs, D), lambda g_, i: (0, 0, 1)),
            vec(D), vec(A_WIDTH), vec(A_WIDTH), vec(C_WIDTH),
            pl.BlockSpec(memory_space=pl.ANY),
        ],
        out_specs=([pl.BlockSpec((1, tm, wd), lambda g_, i: (g_, i, 0)) for wd in widths]
                   + [pl.BlockSpec((1, rs, wd), lambda g_, i: (0, 0, 0)) for wd in widths]),
        out_shape=([jax.ShapeDtypeStruct((G, R, wd), dt) for wd, dt in zip(widths, dtypes_p)]
                   + [jax.ShapeDtypeStruct((1, rs, wd), dt) for wd, dt in zip(widths, dtypes_s)]),
        scratch_shapes=_resident_weight_scratch(D, n_in),
        compiler_params=_cparams(("arbitrary", "arbitrary")),
        name="in_proj",
    )(x, mod, mod, x_s, mod_s, mod_s, norm_g.reshape(1, D), q_gain.reshape(1, A_WIDTH), k_gain.reshape(1, A_WIDTH),
      sgu_gain.reshape(1, C_WIDTH), w_in)


def _attn_body(bias_ref, q_ref, k_ref, v_ref, u_ref, o_ref, acc_ref, run_ref, *, layer, blk):
    hp = pl.program_id(1)
    qi = pl.program_id(2)
    scale2 = HEAD_DIM ** -0.5 * LOG2E
    row = lax.broadcasted_iota(jnp.int32, (blk, blk), 0)
    col = lax.broadcasted_iota(jnp.int32, (blk, blk), 1)
    acc_ref[...] = jnp.zeros_like(acc_ref)
    run_ref[...] = jnp.zeros_like(run_ref)

    def kv_block(j, masked):
        ks = pl.multiple_of(j * blk, blk)
        for e0 in range(0, ATT_HEADS, ATT_GROUP):
            head_group(ks, range(e0, e0 + ATT_GROUP), masked)

    def head_group(ks, heads, masked):
        a_bf, lbs = {}, {}
        for e in heads:
            cols = slice(e * HEAD_DIM, (e + 1) * HEAD_DIM)
            bias2 = bias_ref[layer, hp * ATT_HEADS + e] * LOG2E
            kb = k_ref[0, pl.ds(ks, blk), cols].astype(BF16)
            z2 = lax.dot_general(q_ref[0, :, cols], kb, _NT, preferred_element_type=F32) * scale2 + bias2
            a, lbs[e] = _stick_logs2(z2)
            if masked:
                a = jnp.where(col < row, a, 0.0)
            a_bf[e] = a.astype(BF16)
        rs = {e: jnp.dot(a_bf[e], u_ref[...], preferred_element_type=F32) for e in heads}
        for e in heads:
            cols = slice(e * HEAD_DIM, (e + 1) * HEAD_DIM)
            seen = run_ref[e]
            later = rs[e][:, :blk] + jnp.concatenate([seen] * (blk // LANES), axis=-1)
            w = jnp.exp2(lbs[e] - later)
            if masked:
                w = jnp.where(col < row, w, 0.0)
            run_ref[e] = seen + rs[e][:, blk:]
            vb = v_ref[0, pl.ds(ks, blk), cols].astype(BF16)
            acc_ref[e] += jnp.dot(w.astype(BF16), vb, preferred_element_type=F32)

    kv_block(qi, True)

    def body(it, carry):
        kv_block(qi - 1 - it, False)
        return carry

    lax.fori_loop(0, qi, body, 0)
    for e in range(ATT_HEADS):
        o_ref[0, :, e * HEAD_DIM:(e + 1) * HEAD_DIM] = acc_ref[e].astype(o_ref.dtype)


def _prompt_attention(q, k, v, sb_bias, layer):
    G, R, _ = q.shape
    blk = ATT_BLK
    wd = ATT_HEADS * HEAD_DIM
    return pl.pallas_call(
        functools.partial(_attn_body, layer=layer, blk=blk),
        grid=(G, A_HEADS // ATT_HEADS, R // blk),
        in_specs=[
            pl.BlockSpec(memory_space=pltpu.SMEM),
            pl.BlockSpec((1, blk, wd), lambda b, h, i: (b, i, h)),
            pl.BlockSpec((1, R, wd), lambda b, h, i: (b, 0, h)),
            pl.BlockSpec((1, R, wd), lambda b, h, i: (b, 0, h)),
            pl.BlockSpec((blk, blk + LANES), lambda b, h, i: (0, 0)),
        ],
        out_specs=pl.BlockSpec((1, blk, wd), lambda b, h, i: (b, i, h)),
        out_shape=jax.ShapeDtypeStruct((G, R, A_WIDTH), BF16),
        scratch_shapes=[pltpu.VMEM((ATT_HEADS, blk, HEAD_DIM), F32), pltpu.VMEM((ATT_HEADS, blk, LANES), F32)],
        compiler_params=_cparams(("arbitrary", "arbitrary", "arbitrary")),
        name="prompt_attn",
    )(sb_bias, q, k, v, _suffix_matrix(blk))


def _conv_body(cur_ref, halo_ref, cw_ref, cb_ref, lg_ref, lb_ref, o_ref, xp_ref, xs_ref, *, tt):
    i = pl.program_id(1)
    xp_ref[0:CONV_HALO, :] = jnp.where(i > 0, halo_ref[0], 0.0)
    xp_ref[CONV_HALO:, :] = cur_ref[0]
    span = tt + CONV_HALO - SUBLANES
    for r in range(1, SUBLANES):
        xs_ref[r - 1] = xp_ref[pl.ds(r, span), :]
    first = CONV_HALO - (CONV_WIDTH - 1)
    y = cb_ref[...]
    for w in range(CONV_WIDTH):
        r = (first + w) % SUBLANES
        base = first + w - r
        win = xp_ref[pl.ds(base, tt), :] if r == 0 else xs_ref[r - 1, pl.ds(base, tt), :]
        y = y + cw_ref[w:w + 1, :] * win
    mu = jnp.mean(y, axis=-1, keepdims=True)
    yc = y - mu
    yn = yc * lax.rsqrt(jnp.mean(yc * yc, axis=-1, keepdims=True) + EPS)
    o_ref[0] = _silu(yn * lg_ref[...] + lb_ref[...]).astype(o_ref.dtype)


def _prompt_conv(xg, conv_w, conv_b, ln_g, ln_b):
    G, R, C = xg.shape
    tt = CONV_TT
    hb = tt // CONV_HALO
    row = lambda a: a.reshape(1, C)
    return pl.pallas_call(
        functools.partial(_conv_body, tt=tt),
        grid=(G, R // tt),
        in_specs=[
            pl.BlockSpec((1, tt, C), lambda b, i: (b, i, 0)),
            pl.BlockSpec((1, CONV_HALO, C), lambda b, i: (b, jnp.maximum(i * hb - 1, 0), 0)),
            pl.BlockSpec((CONV_WIDTH, C), lambda b, i: (0, 0)),
            pl.BlockSpec((1, C), lambda b, i: (0, 0)),
            pl.BlockSpec((1, C), lambda b, i: (0, 0)),
            pl.BlockSpec((1, C), lambda b, i: (0, 0)),
        ],
        out_specs=pl.BlockSpec((1, tt, C), lambda b, i: (b, i, 0)),
        out_shape=jax.ShapeDtypeStruct((G, R, C), BF16),
        scratch_shapes=[pltpu.VMEM((tt + CONV_HALO, C), F32),
                        pltpu.VMEM((SUBLANES - 1, tt + CONV_HALO - SUBLANES, C), F32)],
        compiler_params=_cparams(("arbitrary", "arbitrary")),
        name="prompt_conv",
    )(xg, xg, conv_w, row(conv_b), row(ln_g), row(ln_b))


def _sgu_body(u_ref, v_ref, w_ref, b_ref, o_ref, *, tt):
    t = lax.broadcasted_iota(jnp.int32, (CHUNK, CHUNK), 0)
    s = lax.broadcasted_iota(jnp.int32, (CHUNK, CHUNK), 1)
    for hh in range(C_HEADS):
        wc = jnp.where(s <= t, w_ref[hh], 0.0).astype(BF16)
        cols = slice(hh * HEAD_DIM, (hh + 1) * HEAD_DIM)
        for c in range(tt // CHUNK):
            rows = slice(c * CHUNK, (c + 1) * CHUNK)
            mixed = jnp.dot(wc, v_ref[0, rows, cols].astype(BF16), preferred_element_type=F32) + b_ref[:, cols]
            o_ref[0, rows, cols] = (u_ref[0, rows, cols].astype(F32) * mixed).astype(o_ref.dtype)


def _prompt_sgu(u, vn, sgu_w, bias_exp):
    G, R, C = u.shape
    tt = SGU_TT
    return pl.pallas_call(
        functools.partial(_sgu_body, tt=tt),
        grid=(G, R // tt),
        in_specs=[
            pl.BlockSpec((1, tt, C), lambda b, i: (b, i, 0)),
            pl.BlockSpec((1, tt, C), lambda b, i: (b, i, 0)),
            pl.BlockSpec((C_HEADS, CHUNK, CHUNK), lambda b, i: (0, 0, 0)),
            pl.BlockSpec((CHUNK, C), lambda b, i: (0, 0)),
        ],
        out_specs=pl.BlockSpec((1, tt, C), lambda b, i: (b, i, 0)),
        out_shape=jax.ShapeDtypeStruct((G, R, C), BF16),
        compiler_params=_cparams(("arbitrary", "arbitrary")),
        name="prompt_sgu",
    )(u, vn, sgu_w, bias_exp)


def _out_proj_body(oa_ref, ob_ref, oc_ref, x_ref, gate_ref, shift_ref, scale_ref,
                   oas_ref, obs_ref, ocs_ref, xs_ref, gates_ref, shifts_ref, scales_ref, g_ref, w_hbm,
                   o_ref, h_ref, os_ref, hs_ref, wbf_ref, stage_ref, sem, *, layer):
    def project(oa, ob, oc, x, gate, shift, scale, out_ref, hout_ref):
        acc = jnp.dot(oa[0], wbf_ref[0:A_WIDTH, :], preferred_element_type=F32)
        acc = acc + jnp.dot(ob[0], wbf_ref[A_WIDTH:A_WIDTH + B_WIDTH, :], preferred_element_type=F32)
        acc = acc + jnp.dot(oc[0], wbf_ref[A_WIDTH + B_WIDTH:, :], preferred_element_type=F32)
        y = x[0] + gate[0] * acc
        out_ref[0] = y
        hout_ref[0] = _norm_rows(y, g_ref[...], shift[0], scale[0])

    @pl.when(_first_row_tile(0))
    def _():
        _load_weights_bf16(w_hbm, layer, wbf_ref, stage_ref, sem)
        project(oas_ref, obs_ref, ocs_ref, xs_ref, gates_ref, shifts_ref, scales_ref, os_ref, hs_ref)

    project(oa_ref, ob_ref, oc_ref, x_ref, gate_ref, shift_ref, scale_ref, o_ref, h_ref)


def _row_specs(tm, width):
    return pl.BlockSpec((1, tm, width), lambda j, g_, i: (g_, i, 0))


def _sample_specs(rs, width):
    return pl.BlockSpec((1, rs, width), lambda j, g_, i: (0, 0, 0))


def _residual_specs(tm, rm, rs, tn, gb):
    return [
        pl.BlockSpec((1, tm, tn), lambda j, g_, i: (g_, i, j)),
        pl.BlockSpec((1, rm, tn), lambda j, g_, i: (g_, 0, gb + j)),
        pl.BlockSpec((1, rs, tn), lambda j, g_, i: (0, 0, j)),
        pl.BlockSpec((1, rs, tn), lambda j, g_, i: (0, 0, gb + j)),
    ]


def _residual_outs(G, R, rs, D, tm, tn):
    specs = [pl.BlockSpec((1, tm, tn), lambda j, g_, i: (g_, i, j)),
             pl.BlockSpec((1, rs, tn), lambda j, g_, i: (0, 0, j))]
    shapes = [jax.ShapeDtypeStruct((G, R, D), F32), jax.ShapeDtypeStruct((1, rs, D), F32)]
    return specs, shapes


def _out_proj(o_p, o_s, w_out, layer, x, x_s, mod, mod_s, norm_g):
    G, R, D = x.shape
    rm, rs = mod.shape[1], x_s.shape[1]
    tm = TM_OUT
    widths = (A_WIDTH, B_WIDTH, C_WIDTH)
    chunk = lambda rows, c, prompt: pl.BlockSpec((1, rows, D), (lambda g_, i: (g_, 0, c)) if prompt else
                                                 (lambda g_, i: (0, 0, c)))
    rows_p = lambda wd: pl.BlockSpec((1, tm, wd), lambda g_, i: (g_, i, 0))
    rows_s = lambda wd: pl.BlockSpec((1, rs, wd), lambda g_, i: (0, 0, 0))
    return pl.pallas_call(
        functools.partial(_out_proj_body, layer=layer),
        grid=(G, R // tm),
        in_specs=([rows_p(wd) for wd in widths] + [rows_p(D)] + [chunk(rm, c, True) for c in (2, 3, 4)]
                  + [rows_s(wd) for wd in widths] + [rows_s(D)] + [chunk(rs, c, False) for c in (2, 3, 4)]
                  + [pl.BlockSpec((1, D), lambda g_, i: (0, 0)), pl.BlockSpec(memory_space=pl.ANY)]),
        out_specs=[rows_p(D), rows_p(D), rows_s(D), rows_s(D)],
        out_shape=[jax.ShapeDtypeStruct((G, R, D), F32), jax.ShapeDtypeStruct((G, R, D), BF16),
                   jax.ShapeDtypeStruct((1, rs, D), F32), jax.ShapeDtypeStruct((1, rs, D), BF16)],
        scratch_shapes=_resident_weight_scratch(D, D),
        compiler_params=_cparams(("arbitrary", "arbitrary")),
        name="out_proj",
    )(*o_p, x, mod, mod, mod, *o_s, x_s, mod_s, mod_s, mod_s, norm_g.reshape(1, D), w_out)


def _gate_up_body(h_ref, hs_ref, wg_ref, wu_ref, o_ref, os_ref, wg_bf, wu_bf):
    def project(x, out_ref):
        a = jnp.dot(x[0], wg_bf[...], preferred_element_type=F32)
        b = jnp.dot(x[0], wu_bf[...], preferred_element_type=F32)
        out_ref[0] = (_silu(a) * b).astype(out_ref.dtype)

    @pl.when(_first_row_tile(1))
    def _():
        wg_bf[...] = wg_ref[...].astype(BF16)
        wu_bf[...] = wu_ref[...].astype(BF16)
        project(hs_ref, os_ref)

    project(h_ref, o_ref)


def _gate_up(h, h_s, w_gate, w_up, layer, tm):
    G, R, K = h.shape
    rs = h_s.shape[1]
    tn = TN
    wspec = pl.BlockSpec((None, K, tn), lambda j, g_, i: (layer, 0, j))
    return pl.pallas_call(
        _gate_up_body,
        grid=(D_FF // tn, G, R // tm),
        in_specs=[_row_specs(tm, K), _sample_specs(rs, K), wspec, wspec],
        out_specs=[pl.BlockSpec((1, tm, tn), lambda j, g_, i: (g_, i, j)),
                   pl.BlockSpec((1, rs, tn), lambda j, g_, i: (0, 0, j))],
        out_shape=[jax.ShapeDtypeStruct((G, R, D_FF), BF16), jax.ShapeDtypeStruct((1, rs, D_FF), BF16)],
        scratch_shapes=[pltpu.VMEM((K, tn), BF16), pltpu.VMEM((K, tn), BF16)],
        compiler_params=_cparams(("arbitrary", "arbitrary", "arbitrary")),
        name="ffn_gate_up",
    )(h, h_s, w_gate, w_up)


def _down_body(h_ref, hs_ref, w_ref, x_ref, gate_ref, xs_ref, gates_ref, o_ref, os_ref, wbf_ref):
    def project(hid, x, gate, out_ref):
        out_ref[0] = x[0] + gate[0] * jnp.dot(hid[0], wbf_ref[...], preferred_element_type=F32)

    @pl.when(_first_row_tile(1))
    def _():
        wbf_ref[...] = w_ref[...].astype(BF16)
        project(hs_ref, xs_ref, gates_ref, os_ref)

    project(h_ref, x_ref, gate_ref, o_ref)


def _down(hid, hid_s, w_down, layer, x, x_s, mod, mod_s, gate_chunk, tm):
    G, R, D = x.shape
    K = hid.shape[2]
    rm, rs = mod.shape[1], x_s.shape[1]
    tn = TN
    gb = gate_chunk * (D // tn)
    out_specs, out_shape = _residual_outs(G, R, rs, D, tm, tn)
    return pl.pallas_call(
        _down_body,
        grid=(D // tn, G, R // tm),
        in_specs=([_row_specs(tm, K), _sample_specs(rs, K),
                   pl.BlockSpec((None, K, tn), lambda j, g_, i: (layer, 0, j))]
                  + _residual_specs(tm, rm, rs, tn, gb)),
        out_specs=out_specs,
        out_shape=out_shape,
        scratch_shapes=[pltpu.VMEM((K, tn), BF16)],
        compiler_params=_cparams(("arbitrary", "arbitrary", "arbitrary")),
        name="ffn_down",
    )(hid, hid_s, w_down, x, mod, x_s, mod_s)


def _paged_attn_body(pt_ref, q_ref, bias_ref, u_ref, *refs, n_steps):
    del pt_ref
    P = PAGES_PER_STEP
    k_refs = refs[:P]
    v_refs = refs[P:2 * P]
    o_ref = refs[2 * P]
    acc_ref, run_ref = refs[2 * P + 1:]
    t = pl.program_id(1)
    rows = PAGE_SIZE * A_HEADS
    n_blk = rows // LANES

    @pl.when(t == 0)
    def _():
        acc_ref[...] = jnp.zeros_like(acc_ref)
        run_ref[...] = jnp.zeros_like(run_ref)

    scale2 = HEAD_DIM ** -0.5 * LOG2E
    q = q_ref[0]
    z2 = jnp.concatenate(
        [lax.dot_general(q, k_refs[p][...].astype(BF16), _NT, preferred_element_type=F32) for p in range(P)],
        axis=0) * scale2 + bias_ref[...] * LOG2E
    lane = lax.broadcasted_iota(jnp.int32, (P * A_HEADS, rows), 1)
    head = lax.broadcasted_iota(jnp.int32, (P * A_HEADS, rows), 0)
    own = (lane % A_HEADS) == (head % A_HEADS)
    a, lb = _stick_logs2(z2)
    a = jnp.where(own, a, 0.0)
    stacked = jnp.concatenate([a[:, c * LANES:(c + 1) * LANES] for c in range(n_blk)], axis=0)
    r = jnp.dot(stacked.astype(BF16), u_ref[...], preferred_element_type=F32)
    seen = run_ref[...]
    later = []
    for p in range(P):
        blocks = [None] * n_blk
        for c in reversed(range(n_blk)):
            r0 = (c * P + p) * A_HEADS
            blocks[c] = r[r0:r0 + A_HEADS, :LANES] + seen
            seen = seen + r[r0:r0 + A_HEADS, LANES:]
        later.append(jnp.concatenate(blocks, axis=-1))
    run_ref[...] = seen
    w = jnp.where(own, jnp.exp2(lb - jnp.concatenate(later, axis=0)), 0.0)
    acc = acc_ref[...]
    for p in range(P):
        wp = w[p * A_HEADS:(p + 1) * A_HEADS].astype(BF16)
        acc = acc + jnp.dot(wp, v_refs[p][...].astype(BF16), preferred_element_type=F32)
    acc_ref[...] = acc

    @pl.when(t == n_steps - 1)
    def _():
        o_ref[0] = acc


def _paged_attention(q, cache_k, cache_v, page_table, sb_bias_l, layer):
    B, n_pages = page_table.shape
    P = PAGES_PER_STEP
    n_steps = n_pages // P
    rows = PAGE_SIZE * A_HEADS

    def page_spec(p):
        return pl.BlockSpec((None, None, rows, HEAD_DIM),
                            lambda b, t, pt: (layer, pt[b, n_pages - 1 - (t * P + p)], 0, 0))

    grid_spec = pltpu.PrefetchScalarGridSpec(
        num_scalar_prefetch=1,
        grid=(B, n_steps),
        in_specs=[
            pl.BlockSpec((1, A_HEADS, HEAD_DIM), lambda b, t, pt: (b, 0, 0)),
            pl.BlockSpec((P * A_HEADS, 1), lambda b, t, pt: (0, 0)),
            pl.BlockSpec((LANES, 2 * LANES), lambda b, t, pt: (0, 0)),
        ] + [page_spec(p) for p in range(P)] * 2,
        out_specs=pl.BlockSpec((1, A_HEADS, HEAD_DIM), lambda b, t, pt: (b, 0, 0)),
        scratch_shapes=[pltpu.VMEM((A_HEADS, HEAD_DIM), F32), pltpu.VMEM((A_HEADS, LANES), F32)],
    )
    return pl.pallas_call(
        functools.partial(_paged_attn_body, n_steps=n_steps),
        grid_spec=grid_spec,
        out_shape=jax.ShapeDtypeStruct((B, A_HEADS, HEAD_DIM), F32),
        compiler_params=_cparams(("arbitrary", "arbitrary")),
        name="paged_attn",
    )(page_table, q, jnp.tile(sb_bias_l, P).reshape(P * A_HEADS, 1), _suffix_matrix(LANES),
      *([cache_k] * P), *([cache_v] * P))


def _sample_mix_body(xg_ref, st_ref, cw_ref, cb_ref, lg_ref, lb_ref, u_ref, vn_ref, w00_ref, b0_ref,
                     ob_ref, oc_ref):
    hist = CONV_WIDTH - 1
    xg = xg_ref[0]
    y = jnp.sum(st_ref[...] * cw_ref[0:hist, :][None], axis=1) + xg * cw_ref[hist:CONV_WIDTH, :] + cb_ref[...]
    mu = jnp.mean(y, axis=-1, keepdims=True)
    yc = y - mu
    yn = yc * lax.rsqrt(jnp.mean(yc * yc, axis=-1, keepdims=True) + EPS)
    ob_ref[0] = _silu(yn * lg_ref[...] + lb_ref[...]).astype(ob_ref.dtype)
    mixed = vn_ref[0].astype(BF16).astype(F32) * w00_ref[...].astype(BF16).astype(F32) + b0_ref[...]
    oc_ref[0] = (u_ref[0] * mixed).astype(oc_ref.dtype)


def _sample_mix(xg, state, conv_w, conv_b, ln_g, ln_b, u, vn, w00e, b0e):
    _, B, C = xg.shape
    row = lambda a: a.reshape(1, C)
    out = jax.ShapeDtypeStruct((1, B, C), BF16)
    return pl.pallas_call(
        _sample_mix_body,
        out_shape=(out, out),
        compiler_params=pltpu.CompilerParams(vmem_limit_bytes=VMEM_LIMIT),
        name="sample_mix",
    )(xg, state, conv_w, row(conv_b), row(ln_g), row(ln_b), u, vn, row(w00e), row(b0e))


def kernel(x_prompt, x_sample, cache_k, cache_v, state_conv, page_table, c_prompt, c_sample, norm1_g, norm2_g, w_ada, b_ada, w_in, q_norm_g, k_norm_g, sb_bias, conv_w, conv_b, conv_ln_g, conv_ln_b, sgu_norm_g, sgu_w, sgu_b, w_out, w_gate, w_up, w_down):
    b_p, seq, _ = x_prompt.shape
    b_s = x_sample.shape[0]
    n_pool = cache_k.shape[1]
    tm_p = TM_PROMPT

    n_c = b_p + b_s
    pad = (-n_c) % SUBLANES
    c_all = jnp.concatenate([c_prompt, c_sample, jnp.zeros((pad, D_MODEL), F32)], axis=0)
    mod = _ada_mod(c_all, w_ada, b_ada)
    mod_p = mod[:, :b_p].reshape(DEPTH, b_p, 1, N_MOD * D_MODEL)
    mod_s = mod[:, b_p:n_c].reshape(DEPTH, 1, b_s, N_MOD * D_MODEL)

    ck = cache_k.reshape(DEPTH, n_pool, PAGE_SIZE * A_HEADS, HEAD_DIM)
    cv = cache_v.reshape(DEPTH, n_pool, PAGE_SIZE * A_HEADS, HEAD_DIM)

    y_p = x_prompt
    y_s = x_sample.reshape(1, b_s, D_MODEL)
    kp_rows, vp_rows, ks_rows, vs_rows, conv_p, conv_s, sgu_s = [], [], [], [], [], [], []
    for l in range(DEPTH):
        q_gain = jnp.tile(q_norm_g[l], A_HEADS)
        k_gain = jnp.tile(k_norm_g[l], A_HEADS)
        sgu_bias = jnp.repeat(sgu_b[l].T, HEAD_DIM, axis=1)

        (q, k, v, xg, u, vn, q_s, k_s, v_s, xg_s, u_s, vn_s) = _in_proj(
            y_p, y_s, mod_p[l], mod_s[l], norm1_g[l], w_in, l, q_gain, k_gain, sgu_norm_g[l])
        o_a = _prompt_attention(q, k, v, sb_bias, l)
        o_b = _prompt_conv(xg, conv_w[l], conv_b[l], conv_ln_g[l], conv_ln_b[l])
        o_c = _prompt_sgu(u, vn, sgu_w[l], sgu_bias)
        o_as = _paged_attention(q_s.reshape(b_s, A_HEADS, HEAD_DIM), ck, cv, page_table, sb_bias[l], l)
        o_as = o_as.reshape(1, b_s, A_WIDTH).astype(BF16)
        w00e = jnp.repeat(sgu_w[l][:, 0, 0], HEAD_DIM)
        b0e = jnp.repeat(sgu_b[l][:, 0], HEAD_DIM)
        o_bs, o_cs = _sample_mix(xg_s, state_conv[l], conv_w[l], conv_b[l], conv_ln_g[l], conv_ln_b[l],
                                 u_s, vn_s, w00e, b0e)
        y_p, h2, y_s, h2_s = _out_proj((o_a, o_b, o_c), (o_as, o_bs, o_cs), w_out, l, y_p, y_s,
                                       mod_p[l], mod_s[l], norm2_g[l])

        hid, hid_s = _gate_up(h2, h2_s, w_gate, w_up, l, tm_p)
        y_p, y_s = _down(hid, hid_s, w_down, l, y_p, y_s, mod_p[l], mod_s[l], 5, TM_DOWN)

        kp_rows.append(k.reshape(b_p, seq, A_HEADS, HEAD_DIM))
        vp_rows.append(v.reshape(b_p, seq, A_HEADS, HEAD_DIM))
        conv_p.append(xg[:, seq - (CONV_WIDTH - 1):, :])
        ks_rows.append(k_s.reshape(b_s, 1, A_HEADS, HEAD_DIM))
        vs_rows.append(v_s.reshape(b_s, 1, A_HEADS, HEAD_DIM))
        conv_s.append(jnp.concatenate([state_conv[l][:, 1:, :], xg_s.reshape(b_s, 1, B_WIDTH)], axis=1))
        sgu_s.append(vn_s.reshape(b_s, 1, C_WIDTH))

    return (y_p, y_s.reshape(b_s, 1, D_MODEL), jnp.stack(kp_rows), jnp.stack(vp_rows),
            jnp.stack(ks_rows), jnp.stack(vs_rows), jnp.stack(conv_p), jnp.stack(conv_s), jnp.stack(sgu_s))
```

```python
import functools

import jax
import jax.numpy as jnp
from jax import lax
from jax.experimental import pallas as pl
from jax.experimental.pallas import tpu as pltpu

F32 = jnp.float32
BF16 = jnp.bfloat16

D_MODEL = 2048
DEPTH = 4
PAGE_SIZE = 128
HEAD_DIM = 128
A_WIDTH = D_MODEL // 2
A_HEADS = A_WIDTH // HEAD_DIM
B_WIDTH = D_MODEL // 4
C_WIDTH = D_MODEL - A_WIDTH - B_WIDTH
C_HEADS = C_WIDTH // HEAD_DIM
CONV_WIDTH = 31
CHUNK = 128
D_FF = ((8 * D_MODEL // 3 + 255) // 256) * 256
N_MOD = 6
EPS = 1e-6

COL_Q = 0
COL_K = A_WIDTH
COL_V = 2 * A_WIDTH
COL_GLU_A = 3 * A_WIDTH
COL_GLU_G = COL_GLU_A + B_WIDTH
COL_U = COL_GLU_G + B_WIDTH
COL_VC = COL_U + C_WIDTH

LANES = 128
SUBLANES = 8
VMEM_LIMIT = 56 * 1024 * 1024
TM_PROMPT = 1024
TM_DOWN = 256
W_CHUNK_DOWN = 256
TM_IN = 256
TM_OUT = 512
W_CHUNK = 512
TN = 512
ATT_BLK = 256
ATT_HEADS = 8
ATT_GROUP = 4
CONV_TT = 256
CONV_HALO = 32
SGU_TT = 512
PAGES_PER_STEP = 16

_NT = (((1,), (1,)), ((), ()))
LOG2E = 1.4426950408889634


def _cparams(sem):
    return pltpu.CompilerParams(dimension_semantics=sem, vmem_limit_bytes=VMEM_LIMIT)


def _stick_logs2(z2):
    a2 = jnp.maximum(z2, 0.0) + jnp.log(1.0 + jnp.exp2(-jnp.abs(z2))) * LOG2E
    return a2, z2 - a2


def _silu(x):
    return x * jax.nn.sigmoid(x)


def _suffix_matrix(size):
    j = jnp.arange(size)[:, None]
    n = jnp.arange(size + LANES)[None, :]
    return jnp.logical_or(n >= size, j > n).astype(BF16)


def _ada_body(c_ref, w_ref, b_ref, o_ref):
    s = _silu(c_ref[...]).astype(BF16)
    o_ref[0] = jnp.dot(s, w_ref[...].astype(BF16), preferred_element_type=F32) + b_ref[0]


def _ada_mod(c_all, w_ada, b_ada):
    rows = c_all.shape[0]
    tn = 1024
    n_mod = w_ada.shape[2]
    return pl.pallas_call(
        _ada_body,
        grid=(DEPTH, n_mod // tn),
        in_specs=[
            pl.BlockSpec((rows, D_MODEL), lambda l, j: (0, 0)),
            pl.BlockSpec((None, D_MODEL, tn), lambda l, j: (l, 0, j)),
            pl.BlockSpec((1, 1, tn), lambda l, j: (l, 0, j)),
        ],
        out_specs=pl.BlockSpec((1, rows, tn), lambda l, j: (l, 0, j)),
        out_shape=jax.ShapeDtypeStruct((DEPTH, rows, n_mod), F32),
        compiler_params=_cparams(("arbitrary", "arbitrary")),
        name="ada_mod",
    )(c_all, w_ada, b_ada.reshape(DEPTH, 1, n_mod))


def _first_row_tile(g_axis):
    return jnp.logical_and(pl.program_id(g_axis) == 0, pl.program_id(g_axis + 1) == 0)


def _norm_rows(x, g, shift, scale):
    y = x * lax.rsqrt(jnp.mean(x * x, axis=-1, keepdims=True) + EPS)
    return ((y * g) * (1.0 + scale) + shift).astype(BF16)


def _rms(a):
    return a * lax.rsqrt(jnp.mean(a * a, axis=-1, keepdims=True) + EPS)


def _load_weights_bf16(w_hbm, layer, wbf_ref, stage_ref, sem):
    chunk = stage_ref.shape[2]
    n_chunks = wbf_ref.shape[1] // chunk

    def copy(c):
        slot = c % 2
        return pltpu.make_async_copy(w_hbm.at[layer, :, pl.ds(c * chunk, chunk)], stage_ref.at[slot], sem.at[slot])

    copy(0).start()
    for c in range(n_chunks):
        if c + 1 < n_chunks:
            copy(c + 1).start()
        copy(c).wait()
        wbf_ref[:, c * chunk:(c + 1) * chunk] = stage_ref[c % 2].astype(BF16)


def _resident_weight_scratch(K, N, chunk):
    return [pltpu.VMEM((K, N), BF16), pltpu.VMEM((2, K, chunk), F32), pltpu.SemaphoreType.DMA((2,))]


def _in_proj_body(x_ref, shift_ref, scale_ref, xs_ref, shifts_ref, scales_ref, g_ref, qg_ref, kg_ref, sg_ref, w_hbm,
                  *refs, layer):
    outs_p, outs_s = refs[:6], refs[6:12]
    wbf_ref, stage_ref, sem = refs[12:]

    def project(x, shift, scale, outs):
        h = _norm_rows(x, g_ref[...], shift, scale)
        seg = lambda c0, wd: jnp.dot(h, wbf_ref[:, c0:c0 + wd], preferred_element_type=F32)
        q_o, k_o, v_o, xg_o, u_o, vn_o = outs
        for o_ref, c0, gain_ref in ((q_o, COL_Q, qg_ref), (k_o, COL_K, kg_ref)):
            a = seg(c0, A_WIDTH)
            heads = [_rms(a[:, hh * HEAD_DIM:(hh + 1) * HEAD_DIM]) for hh in range(A_HEADS)]
            o_ref[0] = (jnp.concatenate(heads, axis=-1) * gain_ref[...]).astype(o_ref.dtype)
        v_o[0] = seg(COL_V, A_WIDTH).astype(v_o.dtype)
        xg_o[0] = (seg(COL_GLU_A, B_WIDTH) * jax.nn.sigmoid(seg(COL_GLU_G, B_WIDTH))).astype(xg_o.dtype)
        u_o[0] = seg(COL_U, C_WIDTH).astype(u_o.dtype)
        vn_o[0] = (_rms(seg(COL_VC, C_WIDTH)) * sg_ref[...]).astype(vn_o.dtype)

    @pl.when(_first_row_tile(0))
    def _():
        _load_weights_bf16(w_hbm, layer, wbf_ref, stage_ref, sem)
        project(xs_ref[0], shifts_ref[0], scales_ref[0], outs_s)

    project(x_ref[0], shift_ref[0], scale_ref[0], outs_p)


def _in_proj(x, x_s, mod, mod_s, norm_g, w_in, layer, q_gain, k_gain, sgu_gain):
    G, R, D = x.shape
    rm, rs = mod.shape[1], x_s.shape[1]
    tm = TM_IN
    n_in = w_in.shape[2]
    widths = (A_WIDTH, A_WIDTH, A_WIDTH, B_WIDTH, C_WIDTH, C_WIDTH)
    dtypes_p = (BF16, F32, F32, F32, BF16, BF16)
    dtypes_s = (BF16, F32, F32, F32, F32, F32)
    vec = lambda n: pl.BlockSpec((1, n), lambda g_, i: (0, 0))
    return pl.pallas_call(
        functools.partial(_in_proj_body, layer=layer),
        grid=(G, R // tm),
        in_specs=[
            pl.BlockSpec((1, tm, D), lambda g_, i: (g_, i, 0)),
            pl.BlockSpec((1, rm, D), lambda g_, i: (g_, 0, 0)),
            pl.BlockSpec((1, rm, D), lambda g_, i: (g_, 0, 1)),
            pl.BlockSpec((1, rs, D), lambda g_, i: (0, 0, 0)),
            pl.BlockSpec((1, rs, D), lambda g_, i: (0, 0, 0)),
            pl.BlockSpec((1, rs, D), lambda g_, i: (0, 0, 1)),
            vec(D), vec(A_WIDTH), vec(A_WIDTH), vec(C_WIDTH),
            pl.BlockSpec(memory_space=pl.ANY),
        ],
        out_specs=([pl.BlockSpec((1, tm, wd), lambda g_, i: (g_, i, 0)) for wd in widths]
                   + [pl.BlockSpec((1, rs, wd), lambda g_, i: (0, 0, 0)) for wd in widths]),
        out_shape=([jax.ShapeDtypeStruct((G, R, wd), dt) for wd, dt in zip(widths, dtypes_p)]
                   + [jax.ShapeDtypeStruct((1, rs, wd), dt) for wd, dt in zip(widths, dtypes_s)]),
        scratch_shapes=_resident_weight_scratch(D, n_in, W_CHUNK),
        compiler_params=_cparams(("arbitrary", "arbitrary")),
        name="in_proj",
    )(x, mod, mod, x_s, mod_s, mod_s, norm_g.reshape(1, D), q_gain.reshape(1, A_WIDTH), k_gain.reshape(1, A_WIDTH),
      sgu_gain.reshape(1, C_WIDTH), w_in)


def _attn_body(bias_ref, q_ref, k_ref, v_ref, u_ref, o_ref, acc_ref, run_ref, *, layer, blk):
    hp = pl.program_id(1)
    qi = pl.program_id(2)
    scale2 = HEAD_DIM ** -0.5 * LOG2E
    row = lax.broadcasted_iota(jnp.int32, (blk, blk), 0)
    col = lax.broadcasted_iota(jnp.int32, (blk, blk), 1)
    acc_ref[...] = jnp.zeros_like(acc_ref)
    run_ref[...] = jnp.zeros_like(run_ref)

    def kv_block(j, masked):
        ks = pl.multiple_of(j * blk, blk)
        for e0 in range(0, ATT_HEADS, ATT_GROUP):
            head_group(ks, range(e0, e0 + ATT_GROUP), masked)

    def head_group(ks, heads, masked):
        a_bf, lbs = {}, {}
        for e in heads:
            cols = slice(e * HEAD_DIM, (e + 1) * HEAD_DIM)
            bias2 = bias_ref[layer, hp * ATT_HEADS + e] * LOG2E
            kb = k_ref[0, pl.ds(ks, blk), cols].astype(BF16)
            z2 = lax.dot_general(q_ref[0, :, cols], kb, _NT, preferred_element_type=F32) * scale2 + bias2
            a, lbs[e] = _stick_logs2(z2)
            if masked:
                a = jnp.where(col < row, a, 0.0)
            a_bf[e] = a.astype(BF16)
        rs = {e: jnp.dot(a_bf[e], u_ref[...], preferred_element_type=F32) for e in heads}
        for e in heads:
            cols = slice(e * HEAD_DIM, (e + 1) * HEAD_DIM)
            seen = run_ref[e]
            later = rs[e][:, :blk] + jnp.concatenate([seen] * (blk // LANES), axis=-1)
            w = jnp.exp2(lbs[e] - later)
            if masked:
                w = jnp.where(col < row, w, 0.0)
            run_ref[e] = seen + rs[e][:, blk:]
            vb = v_ref[0, pl.ds(ks, blk), cols].astype(BF16)
            acc_ref[e] += jnp.dot(w.astype(BF16), vb, preferred_element_type=F32)

    kv_block(qi, True)

    def body(it, carry):
        kv_block(qi - 1 - it, False)
        return carry

    lax.fori_loop(0, qi, body, 0)
    for e in range(ATT_HEADS):
        o_ref[0, :, e * HEAD_DIM:(e + 1) * HEAD_DIM] = acc_ref[e].astype(o_ref.dtype)


def _prompt_attention(q, k, v, sb_bias, layer):
    G, R, _ = q.shape
    blk = ATT_BLK
    wd = ATT_HEADS * HEAD_DIM
    return pl.pallas_call(
        functools.partial(_attn_body, layer=layer, blk=blk),
        grid=(G, A_HEADS // ATT_HEADS, R // blk),
        in_specs=[
            pl.BlockSpec(memory_space=pltpu.SMEM),
            pl.BlockSpec((1, blk, wd), lambda b, h, i: (b, i, h)),
            pl.BlockSpec((1, R, wd), lambda b, h, i: (b, 0, h)),
            pl.BlockSpec((1, R, wd), lambda b, h, i: (b, 0, h)),
            pl.BlockSpec((blk, blk + LANES), lambda b, h, i: (0, 0)),
        ],
        out_specs=pl.BlockSpec((1, blk, wd), lambda b, h, i: (b, i, h)),
        out_shape=jax.ShapeDtypeStruct((G, R, A_WIDTH), BF16),
        scratch_shapes=[pltpu.VMEM((ATT_HEADS, blk, HEAD_DIM), F32), pltpu.VMEM((ATT_HEADS, blk, LANES), F32)],
        compiler_params=_cparams(("arbitrary", "arbitrary", "arbitrary")),
        name="prompt_attn",
    )(sb_bias, q, k, v, _suffix_matrix(blk))


def _conv_body(cur_ref, halo_ref, cw_ref, cb_ref, lg_ref, lb_ref, o_ref, xp_ref, xs_ref, *, tt):
    i = pl.program_id(1)
    xp_ref[0:CONV_HALO, :] = jnp.where(i > 0, halo_ref[0], 0.0)
    xp_ref[CONV_HALO:, :] = cur_ref[0]
    span = tt + CONV_HALO - SUBLANES
    for r in range(1, SUBLANES):
        xs_ref[r - 1] = xp_ref[pl.ds(r, span), :]
    first = CONV_HALO - (CONV_WIDTH - 1)
    y = cb_ref[...]
    for w in range(CONV_WIDTH):
        r = (first + w) % SUBLANES
        base = first + w - r
        win = xp_ref[pl.ds(base, tt), :] if r == 0 else xs_ref[r - 1, pl.ds(base, tt), :]
        y = y + cw_ref[w:w + 1, :] * win
    mu = jnp.mean(y, axis=-1, keepdims=True)
    yc = y - mu
    yn = yc * lax.rsqrt(jnp.mean(yc * yc, axis=-1, keepdims=True) + EPS)
    o_ref[0] = _silu(yn * lg_ref[...] + lb_ref[...]).astype(o_ref.dtype)


def _prompt_conv(xg, conv_w, conv_b, ln_g, ln_b):
    G, R, C = xg.shape
    tt = CONV_TT
    hb = tt // CONV_HALO
    row = lambda a: a.reshape(1, C)
    return pl.pallas_call(
        functools.partial(_conv_body, tt=tt),
        grid=(G, R // tt),
        in_specs=[
            pl.BlockSpec((1, tt, C), lambda b, i: (b, i, 0)),
            pl.BlockSpec((1, CONV_HALO, C), lambda b, i: (b, jnp.maximum(i * hb - 1, 0), 0)),
            pl.BlockSpec((CONV_WIDTH, C), lambda b, i: (0, 0)),
            pl.BlockSpec((1, C), lambda b, i: (0, 0)),
            pl.BlockSpec((1, C), lambda b, i: (0, 0)),
            pl.BlockSpec((1, C), lambda b, i: (0, 0)),
        ],
        out_specs=pl.BlockSpec((1, tt, C), lambda b, i: (b, i, 0)),
        out_shape=jax.ShapeDtypeStruct((G, R, C), BF16),
        scratch_shapes=[pltpu.VMEM((tt + CONV_HALO, C), F32),
                        pltpu.VMEM((SUBLANES - 1, tt + CONV_HALO - SUBLANES, C), F32)],
        compiler_params=_cparams(("arbitrary", "arbitrary")),
        name="prompt_conv",
    )(xg, xg, conv_w, row(conv_b), row(ln_g), row(ln_b))


def _sgu_body(u_ref, v_ref, w_ref, b_ref, o_ref, *, tt):
    t = lax.broadcasted_iota(jnp.int32, (CHUNK, CHUNK), 0)
    s = lax.broadcasted_iota(jnp.int32, (CHUNK, CHUNK), 1)
    for hh in range(C_HEADS):
        wc = jnp.where(s <= t, w_ref[hh], 0.0).astype(BF16)
        cols = slice(hh * HEAD_DIM, (hh + 1) * HEAD_DIM)
        for c in range(tt // CHUNK):
            rows = slice(c * CHUNK, (c + 1) * CHUNK)
            mixed = jnp.dot(wc, v_ref[0, rows, cols].astype(BF16), preferred_element_type=F32) + b_ref[:, cols]
            o_ref[0, rows, cols] = (u_ref[0, rows, cols].astype(F32) * mixed).astype(o_ref.dtype)


def _prompt_sgu(u, vn, sgu_w, bias_exp):
    G, R, C = u.shape
    tt = SGU_TT
    return pl.pallas_call(
        functools.partial(_sgu_body, tt=tt),
        grid=(G, R // tt),
        in_specs=[
            pl.BlockSpec((1, tt, C), lambda b, i: (b, i, 0)),
            pl.BlockSpec((1, tt, C), lambda b, i: (b, i, 0)),
            pl.BlockSpec((C_HEADS, CHUNK, CHUNK), lambda b, i: (0, 0, 0)),
            pl.BlockSpec((CHUNK, C), lambda b, i: (0, 0)),
        ],
        out_specs=pl.BlockSpec((1, tt, C), lambda b, i: (b, i, 0)),
        out_shape=jax.ShapeDtypeStruct((G, R, C), BF16),
        compiler_params=_cparams(("arbitrary", "arbitrary")),
        name="prompt_sgu",
    )(u, vn, sgu_w, bias_exp)


def _out_proj_body(oa_ref, ob_ref, oc_ref, x_ref, gate_ref, shift_ref, scale_ref,
                   oas_ref, obs_ref, ocs_ref, xs_ref, gates_ref, shifts_ref, scales_ref, g_ref, w_hbm,
                   o_ref, h_ref, os_ref, hs_ref, wbf_ref, stage_ref, sem, *, layer):
    def project(oa, ob, oc, x, gate, shift, scale, out_ref, hout_ref):
        acc = jnp.dot(oa[0], wbf_ref[0:A_WIDTH, :], preferred_element_type=F32)
        acc = acc + jnp.dot(ob[0], wbf_ref[A_WIDTH:A_WIDTH + B_WIDTH, :], preferred_element_type=F32)
        acc = acc + jnp.dot(oc[0], wbf_ref[A_WIDTH + B_WIDTH:, :], preferred_element_type=F32)
        y = x[0] + gate[0] * acc
        out_ref[0] = y
        hout_ref[0] = _norm_rows(y, g_ref[...], shift[0], scale[0])

    @pl.when(_first_row_tile(0))
    def _():
        _load_weights_bf16(w_hbm, layer, wbf_ref, stage_ref, sem)
        project(oas_ref, obs_ref, ocs_ref, xs_ref, gates_ref, shifts_ref, scales_ref, os_ref, hs_ref)

    project(oa_ref, ob_ref, oc_ref, x_ref, gate_ref, shift_ref, scale_ref, o_ref, h_ref)


def _row_specs(tm, width):
    return pl.BlockSpec((1, tm, width), lambda j, g_, i: (g_, i, 0))


def _sample_specs(rs, width):
    return pl.BlockSpec((1, rs, width), lambda j, g_, i: (0, 0, 0))


def _out_proj(o_p, o_s, w_out, layer, x, x_s, mod, mod_s, norm_g):
    G, R, D = x.shape
    rm, rs = mod.shape[1], x_s.shape[1]
    tm = TM_OUT
    widths = (A_WIDTH, B_WIDTH, C_WIDTH)
    chunk = lambda rows, c, prompt: pl.BlockSpec((1, rows, D), (lambda g_, i: (g_, 0, c)) if prompt else
                                                 (lambda g_, i: (0, 0, c)))
    rows_p = lambda wd: pl.BlockSpec((1, tm, wd), lambda g_, i: (g_, i, 0))
    rows_s = lambda wd: pl.BlockSpec((1, rs, wd), lambda g_, i: (0, 0, 0))
    return pl.pallas_call(
        functools.partial(_out_proj_body, layer=layer),
        grid=(G, R // tm),
        in_specs=([rows_p(wd) for wd in widths] + [rows_p(D)] + [chunk(rm, c, True) for c in (2, 3, 4)]
                  + [rows_s(wd) for wd in widths] + [rows_s(D)] + [chunk(rs, c, False) for c in (2, 3, 4)]
                  + [pl.BlockSpec((1, D), lambda g_, i: (0, 0)), pl.BlockSpec(memory_space=pl.ANY)]),
        out_specs=[rows_p(D), rows_p(D), rows_s(D), rows_s(D)],
        out_shape=[jax.ShapeDtypeStruct((G, R, D), F32), jax.ShapeDtypeStruct((G, R, D), BF16),
                   jax.ShapeDtypeStruct((1, rs, D), F32), jax.ShapeDtypeStruct((1, rs, D), BF16)],
        scratch_shapes=_resident_weight_scratch(D, D, W_CHUNK),
        compiler_params=_cparams(("arbitrary", "arbitrary")),
        name="out_proj",
    )(*o_p, x, mod, mod, mod, *o_s, x_s, mod_s, mod_s, mod_s, norm_g.reshape(1, D), w_out)


def _gate_up_body(h_ref, hs_ref, wg_ref, wu_ref, o_ref, os_ref, wg_bf, wu_bf):
    def project(x, out_ref):
        a = jnp.dot(x[0], wg_bf[...], preferred_element_type=F32)
        b = jnp.dot(x[0], wu_bf[...], preferred_element_type=F32)
        out_ref[0] = (_silu(a) * b).astype(out_ref.dtype)

    @pl.when(_first_row_tile(1))
    def _():
        wg_bf[...] = wg_ref[...].astype(BF16)
        wu_bf[...] = wu_ref[...].astype(BF16)
        project(hs_ref, os_ref)

    project(h_ref, o_ref)


def _gate_up(h, h_s, w_gate, w_up, layer, tm):
    G, R, K = h.shape
    rs = h_s.shape[1]
    tn = TN
    wspec = pl.BlockSpec((None, K, tn), lambda j, g_, i: (layer, 0, j))
    return pl.pallas_call(
        _gate_up_body,
        grid=(D_FF // tn, G, R // tm),
        in_specs=[_row_specs(tm, K), _sample_specs(rs, K), wspec, wspec],
        out_specs=[pl.BlockSpec((1, tm, tn), lambda j, g_, i: (g_, i, j)),
                   pl.BlockSpec((1, rs, tn), lambda j, g_, i: (0, 0, j))],
        out_shape=[jax.ShapeDtypeStruct((G, R, D_FF), BF16), jax.ShapeDtypeStruct((1, rs, D_FF), BF16)],
        scratch_shapes=[pltpu.VMEM((K, tn), BF16), pltpu.VMEM((K, tn), BF16)],
        compiler_params=_cparams(("arbitrary", "arbitrary", "arbitrary")),
        name="ffn_gate_up",
    )(h, h_s, w_gate, w_up)


def _down_body(h_ref, x_ref, gate_ref, hs_ref, xs_ref, gates_ref, w_hbm, o_ref, os_ref, wbf_ref, stage_ref, sem,
               *, layer):
    def project(hid, x, gate, out_ref):
        out_ref[0] = x[0] + gate[0] * jnp.dot(hid[0], wbf_ref[...], preferred_element_type=F32)

    @pl.when(_first_row_tile(0))
    def _():
        _load_weights_bf16(w_hbm, layer, wbf_ref, stage_ref, sem)
        project(hs_ref, xs_ref, gates_ref, os_ref)

    project(h_ref, x_ref, gate_ref, o_ref)


def _down(hid, hid_s, w_down, layer, x, x_s, mod, mod_s, gate_chunk):
    G, R, D = x.shape
    K = hid.shape[2]
    rm, rs = mod.shape[1], x_s.shape[1]
    tm = TM_DOWN
    return pl.pallas_call(
        functools.partial(_down_body, layer=layer),
        grid=(G, R // tm),
        in_specs=[
            pl.BlockSpec((1, tm, K), lambda g_, i: (g_, i, 0)),
            pl.BlockSpec((1, tm, D), lambda g_, i: (g_, i, 0)),
            pl.BlockSpec((1, rm, D), lambda g_, i: (g_, 0, gate_chunk)),
            pl.BlockSpec((1, rs, K), lambda g_, i: (0, 0, 0)),
            pl.BlockSpec((1, rs, D), lambda g_, i: (0, 0, 0)),
            pl.BlockSpec((1, rs, D), lambda g_, i: (0, 0, gate_chunk)),
            pl.BlockSpec(memory_space=pl.ANY),
        ],
        out_specs=[pl.BlockSpec((1, tm, D), lambda g_, i: (g_, i, 0)),
                   pl.BlockSpec((1, rs, D), lambda g_, i: (0, 0, 0))],
        out_shape=[jax.ShapeDtypeStruct((G, R, D), F32), jax.ShapeDtypeStruct((1, rs, D), F32)],
        scratch_shapes=_resident_weight_scratch(K, D, W_CHUNK_DOWN),
        compiler_params=_cparams(("arbitrary", "arbitrary")),
        name="ffn_down",
    )(hid, x, mod, hid_s, x_s, mod_s, w_down)


def _paged_attn_body(pt_ref, q_ref, bias_ref, u_ref, *refs, n_steps):
    del pt_ref
    P = PAGES_PER_STEP
    k_refs = refs[:P]
    v_refs = refs[P:2 * P]
    o_ref = refs[2 * P]
    acc_ref, run_ref = refs[2 * P + 1:]
    t = pl.program_id(1)
    rows = PAGE_SIZE * A_HEADS
    n_blk = rows // LANES

    @pl.when(t == 0)
    def _():
        acc_ref[...] = jnp.zeros_like(acc_ref)
        run_ref[...] = jnp.zeros_like(run_ref)

    scale2 = HEAD_DIM ** -0.5 * LOG2E
    q = q_ref[0]
    z2 = jnp.concatenate(
        [lax.dot_general(q, k_refs[p][...].astype(BF16), _NT, preferred_element_type=F32) for p in range(P)],
        axis=0) * scale2 + bias_ref[...] * LOG2E
    lane = lax.broadcasted_iota(jnp.int32, (P * A_HEADS, rows), 1)
    head = lax.broadcasted_iota(jnp.int32, (P * A_HEADS, rows), 0)
    own = (lane % A_HEADS) == (head % A_HEADS)
    a, lb = _stick_logs2(z2)
    a = jnp.where(own, a, 0.0)
    stacked = jnp.concatenate([a[:, c * LANES:(c + 1) * LANES] for c in range(n_blk)], axis=0)
    r = jnp.dot(stacked.astype(BF16), u_ref[...], preferred_element_type=F32)
    seen = run_ref[...]
    later = []
    for p in range(P):
        blocks = [None] * n_blk
        for c in reversed(range(n_blk)):
            r0 = (c * P + p) * A_HEADS
            blocks[c] = r[r0:r0 + A_HEADS, :LANES] + seen
            seen = seen + r[r0:r0 + A_HEADS, LANES:]
        later.append(jnp.concatenate(blocks, axis=-1))
    run_ref[...] = seen
    w = jnp.where(own, jnp.exp2(lb - jnp.concatenate(later, axis=0)), 0.0)
    acc = acc_ref[...]
    for p in range(P):
        wp = w[p * A_HEADS:(p + 1) * A_HEADS].astype(BF16)
        acc = acc + jnp.dot(wp, v_refs[p][...].astype(BF16), preferred_element_type=F32)
    acc_ref[...] = acc

    @pl.when(t == n_steps - 1)
    def _():
        o_ref[0] = acc


def _paged_attention(q, cache_k, cache_v, page_table, sb_bias_l, layer):
    B, n_pages = page_table.shape
    P = PAGES_PER_STEP
    n_steps = n_pages // P
    rows = PAGE_SIZE * A_HEADS

    def page_spec(p):
        return pl.BlockSpec((None, None, rows, HEAD_DIM),
                            lambda b, t, pt: (layer, pt[b, n_pages - 1 - (t * P + p)], 0, 0))

    grid_spec = pltpu.PrefetchScalarGridSpec(
        num_scalar_prefetch=1,
        grid=(B, n_steps),
        in_specs=[
            pl.BlockSpec((1, A_HEADS, HEAD_DIM), lambda b, t, pt: (b, 0, 0)),
            pl.BlockSpec((P * A_HEADS, 1), lambda b, t, pt: (0, 0)),
            pl.BlockSpec((LANES, 2 * LANES), lambda b, t, pt: (0, 0)),
        ] + [page_spec(p) for p in range(P)] * 2,
        out_specs=pl.BlockSpec((1, A_HEADS, HEAD_DIM), lambda b, t, pt: (b, 0, 0)),
        scratch_shapes=[pltpu.VMEM((A_HEADS, HEAD_DIM), F32), pltpu.VMEM((A_HEADS, LANES), F32)],
    )
    return pl.pallas_call(
        functools.partial(_paged_attn_body, n_steps=n_steps),
        grid_spec=grid_spec,
        out_shape=jax.ShapeDtypeStruct((B, A_HEADS, HEAD_DIM), F32),
        compiler_params=_cparams(("arbitrary", "arbitrary")),
        name="paged_attn",
    )(page_table, q, jnp.tile(sb_bias_l, P).reshape(P * A_HEADS, 1), _suffix_matrix(LANES),
      *([cache_k] * P), *([cache_v] * P))


def _sample_mix_body(xg_ref, st_ref, cw_ref, cb_ref, lg_ref, lb_ref, u_ref, vn_ref, w00_ref, b0_ref,
                     ob_ref, oc_ref):
    hist = CONV_WIDTH - 1
    xg = xg_ref[0]
    y = jnp.sum(st_ref[...] * cw_ref[0:hist, :][None], axis=1) + xg * cw_ref[hist:CONV_WIDTH, :] + cb_ref[...]
    mu = jnp.mean(y, axis=-1, keepdims=True)
    yc = y - mu
    yn = yc * lax.rsqrt(jnp.mean(yc * yc, axis=-1, keepdims=True) + EPS)
    ob_ref[0] = _silu(yn * lg_ref[...] + lb_ref[...]).astype(ob_ref.dtype)
    mixed = vn_ref[0].astype(BF16).astype(F32) * w00_ref[...].astype(BF16).astype(F32) + b0_ref[...]
    oc_ref[0] = (u_ref[0] * mixed).astype(oc_ref.dtype)


def _sample_mix(xg, state, conv_w, conv_b, ln_g, ln_b, u, vn, w00e, b0e):
    _, B, C = xg.shape
    row = lambda a: a.reshape(1, C)
    out = jax.ShapeDtypeStruct((1, B, C), BF16)
    return pl.pallas_call(
        _sample_mix_body,
        out_shape=(out, out),
        compiler_params=pltpu.CompilerParams(vmem_limit_bytes=VMEM_LIMIT),
        name="sample_mix",
    )(xg, state, conv_w, row(conv_b), row(ln_g), row(ln_b), u, vn, row(w00e), row(b0e))


def kernel(x_prompt, x_sample, cache_k, cache_v, state_conv, page_table, c_prompt, c_sample, norm1_g, norm2_g, w_ada, b_ada, w_in, q_norm_g, k_norm_g, sb_bias, conv_w, conv_b, conv_ln_g, conv_ln_b, sgu_norm_g, sgu_w, sgu_b, w_out, w_gate, w_up, w_down):
    b_p, seq, _ = x_prompt.shape
    b_s = x_sample.shape[0]
    n_pool = cache_k.shape[1]
    tm_p = TM_PROMPT

    n_c = b_p + b_s
    pad = (-n_c) % SUBLANES
    c_all = jnp.concatenate([c_prompt, c_sample, jnp.zeros((pad, D_MODEL), F32)], axis=0)
    mod = _ada_mod(c_all, w_ada, b_ada)
    mod_p = mod[:, :b_p].reshape(DEPTH, b_p, 1, N_MOD * D_MODEL)
    mod_s = mod[:, b_p:n_c].reshape(DEPTH, 1, b_s, N_MOD * D_MODEL)

    ck = cache_k.reshape(DEPTH, n_pool, PAGE_SIZE * A_HEADS, HEAD_DIM)
    cv = cache_v.reshape(DEPTH, n_pool, PAGE_SIZE * A_HEADS, HEAD_DIM)

    y_p = x_prompt
    y_s = x_sample.reshape(1, b_s, D_MODEL)
    kp_rows, vp_rows, ks_rows, vs_rows, conv_p, conv_s, sgu_s = [], [], [], [], [], [], []
    for l in range(DEPTH):
        q_gain = jnp.tile(q_norm_g[l], A_HEADS)
        k_gain = jnp.tile(k_norm_g[l], A_HEADS)
        sgu_bias = jnp.repeat(sgu_b[l].T, HEAD_DIM, axis=1)

        (q, k, v, xg, u, vn, q_s, k_s, v_s, xg_s, u_s, vn_s) = _in_proj(
            y_p, y_s, mod_p[l], mod_s[l], norm1_g[l], w_in, l, q_gain, k_gain, sgu_norm_g[l])
        o_a = _prompt_attention(q, k, v, sb_bias, l)
        o_b = _prompt_conv(xg, conv_w[l], conv_b[l], conv_ln_g[l], conv_ln_b[l])
        o_c = _prompt_sgu(u, vn, sgu_w[l], sgu_bias)
        o_as = _paged_attention(q_s.reshape(b_s, A_HEADS, HEAD_DIM), ck, cv, page_table, sb_bias[l], l)
        o_as = o_as.reshape(1, b_s, A_WIDTH).astype(BF16)
        w00e = jnp.repeat(sgu_w[l][:, 0, 0], HEAD_DIM)
        b0e = jnp.repeat(sgu_b[l][:, 0], HEAD_DIM)
        o_bs, o_cs = _sample_mix(xg_s, state_conv[l], conv_w[l], conv_b[l], conv_ln_g[l], conv_ln_b[l],
                                 u_s, vn_s, w00e, b0e)
        y_p, h2, y_s, h2_s = _out_proj((o_a, o_b, o_c), (o_as, o_bs, o_cs), w_out, l, y_p, y_s,
                                       mod_p[l], mod_s[l], norm2_g[l])

        hid, hid_s = _gate_up(h2, h2_s, w_gate, w_up, l, tm_p)
        y_p, y_s = _down(hid, hid_s, w_down, l, y_p, y_s, mod_p[l], mod_s[l], 5)

        kp_rows.append(k.reshape(b_p, seq, A_HEADS, HEAD_DIM))
        vp_rows.append(v.reshape(b_p, seq, A_HEADS, HEAD_DIM))
        conv_p.append(xg[:, seq - (CONV_WIDTH - 1):, :])
        ks_rows.append(k_s.reshape(b_s, 1, A_HEADS, HEAD_DIM))
        vs_rows.append(v_s.reshape(b_s, 1, A_HEADS, HEAD_DIM))
        conv_s.append(jnp.concatenate([state_conv[l][:, 1:, :], xg_s.reshape(b_s, 1, B_WIDTH)], axis=1))
        sgu_s.append(vn_s.reshape(b_s, 1, C_WIDTH))

    return (y_p, y_s.reshape(b_s, 1, D_MODEL), jnp.stack(kp_rows), jnp.stack(vp_rows),
            jnp.stack(ks_rows), jnp.stack(vs_rows), jnp.stack(conv_p), jnp.stack(conv_s), jnp.stack(sgu_s))
```

```python
import functools

import jax
import jax.numpy as jnp
from jax import lax
from jax.experimental import pallas as pl
from jax.experimental.pallas import tpu as pltpu

F32 = jnp.float32
BF16 = jnp.bfloat16

D_MODEL = 2048
DEPTH = 4
PAGE_SIZE = 128
HEAD_DIM = 128
A_WIDTH = D_MODEL // 2
A_HEADS = A_WIDTH // HEAD_DIM
B_WIDTH = D_MODEL // 4
C_WIDTH = D_MODEL - A_WIDTH - B_WIDTH
C_HEADS = C_WIDTH // HEAD_DIM
CONV_WIDTH = 31
CHUNK = 128
D_FF = ((8 * D_MODEL // 3 + 255) // 256) * 256
N_MOD = 6
EPS = 1e-6

COL_Q = 0
COL_K = A_WIDTH
COL_V = 2 * A_WIDTH
COL_GLU_A = 3 * A_WIDTH
COL_GLU_G = COL_GLU_A + B_WIDTH
COL_U = COL_GLU_G + B_WIDTH
COL_VC = COL_U + C_WIDTH

LANES = 128
SUBLANES = 8
VMEM_LIMIT = 56 * 1024 * 1024
TM_PROMPT = 1024
TM_DOWN = 256
W_CHUNK_DOWN = 256
TM_IN = 256
TM_OUT = 512
OUT_SUB_ROWS = 128
W_CHUNK = 512
TN = 512
ATT_BLK = 256
ATT_HEADS = 8
ATT_GROUP = 4
CONV_TT = 256
CONV_HALO = 32
SGU_TT = 512
PAGES_PER_STEP = 16

_NT = (((1,), (1,)), ((), ()))
LOG2E = 1.4426950408889634
Q_SCALE2 = HEAD_DIM ** -0.5 * LOG2E


def _cparams(sem):
    return pltpu.CompilerParams(dimension_semantics=sem, vmem_limit_bytes=VMEM_LIMIT)


def _stick_logs2(z2):
    a2 = jnp.maximum(z2, 0.0) + jnp.log(1.0 + jnp.exp2(-jnp.abs(z2))) * LOG2E
    return a2, z2 - a2


def _silu(x):
    return x * jax.nn.sigmoid(x)


def _suffix_matrix(size):
    j = jnp.arange(size)[:, None]
    n = jnp.arange(size + LANES)[None, :]
    return jnp.logical_or(n >= size, j > n).astype(BF16)


def _ada_body(c_ref, w_ref, b_ref, o_ref):
    s = _silu(c_ref[...]).astype(BF16)
    o_ref[0] = jnp.dot(s, w_ref[...].astype(BF16), preferred_element_type=F32) + b_ref[0]


def _ada_mod(c_all, w_ada, b_ada):
    rows = c_all.shape[0]
    tn = 1024
    n_mod = w_ada.shape[2]
    return pl.pallas_call(
        _ada_body,
        grid=(DEPTH, n_mod // tn),
        in_specs=[
            pl.BlockSpec((rows, D_MODEL), lambda l, j: (0, 0)),
            pl.BlockSpec((None, D_MODEL, tn), lambda l, j: (l, 0, j)),
            pl.BlockSpec((1, 1, tn), lambda l, j: (l, 0, j)),
        ],
        out_specs=pl.BlockSpec((1, rows, tn), lambda l, j: (l, 0, j)),
        out_shape=jax.ShapeDtypeStruct((DEPTH, rows, n_mod), F32),
        compiler_params=_cparams(("arbitrary", "arbitrary")),
        name="ada_mod",
    )(c_all, w_ada, b_ada.reshape(DEPTH, 1, n_mod))


def _first_row_tile(g_axis):
    return jnp.logical_and(pl.program_id(g_axis) == 0, pl.program_id(g_axis + 1) == 0)


def _norm_rows(x, g, shift, scale):
    y = x * lax.rsqrt(jnp.mean(x * x, axis=-1, keepdims=True) + EPS)
    return ((y * g) * (1.0 + scale) + shift).astype(BF16)


def _rms(a):
    return a * lax.rsqrt(jnp.mean(a * a, axis=-1, keepdims=True) + EPS)


def _load_weights_bf16(w_hbm, layer, wbf_ref, stage_ref, sem):
    chunk = stage_ref.shape[2]
    n_chunks = wbf_ref.shape[1] // chunk

    def copy(c):
        slot = c % 2
        return pltpu.make_async_copy(w_hbm.at[layer, :, pl.ds(c * chunk, chunk)], stage_ref.at[slot], sem.at[slot])

    copy(0).start()
    for c in range(n_chunks):
        if c + 1 < n_chunks:
            copy(c + 1).start()
        copy(c).wait()
        wbf_ref[:, c * chunk:(c + 1) * chunk] = stage_ref[c % 2].astype(BF16)


def _resident_weight_scratch(K, N, chunk):
    return [pltpu.VMEM((K, N), BF16), pltpu.VMEM((2, K, chunk), F32), pltpu.SemaphoreType.DMA((2,))]


def _in_proj_body(x_ref, shift_ref, scale_ref, xs_ref, shifts_ref, scales_ref, g_ref, qg_ref, kg_ref, sg_ref, w_hbm,
                  *refs, layer):
    outs_p, outs_s = refs[:6], refs[6:12]
    wbf_ref, stage_ref, sem = refs[12:]

    def project(x, shift, scale, outs):
        h = _norm_rows(x, g_ref[...], shift, scale)
        seg = lambda c0, wd: jnp.dot(h, wbf_ref[:, c0:c0 + wd], preferred_element_type=F32)
        q_o, k_o, v_o, xg_o, u_o, vn_o = outs
        for o_ref, c0, gain_ref, post in ((q_o, COL_Q, qg_ref, Q_SCALE2), (k_o, COL_K, kg_ref, None)):
            a = seg(c0, A_WIDTH)
            heads = [_rms(a[:, hh * HEAD_DIM:(hh + 1) * HEAD_DIM]) for hh in range(A_HEADS)]
            normed = jnp.concatenate(heads, axis=-1) * gain_ref[...]
            o_ref[0] = (normed if post is None else normed * post).astype(o_ref.dtype)
        v_o[0] = seg(COL_V, A_WIDTH).astype(v_o.dtype)
        xg_o[0] = (seg(COL_GLU_A, B_WIDTH) * jax.nn.sigmoid(seg(COL_GLU_G, B_WIDTH))).astype(xg_o.dtype)
        u_o[0] = seg(COL_U, C_WIDTH).astype(u_o.dtype)
        vn_o[0] = (_rms(seg(COL_VC, C_WIDTH)) * sg_ref[...]).astype(vn_o.dtype)

    @pl.when(_first_row_tile(0))
    def _():
        _load_weights_bf16(w_hbm, layer, wbf_ref, stage_ref, sem)
        project(xs_ref[0], shifts_ref[0], scales_ref[0], outs_s)

    project(x_ref[0], shift_ref[0], scale_ref[0], outs_p)


def _in_proj(x, x_s, mod, mod_s, norm_g, w_in, layer, q_gain, k_gain, sgu_gain):
    G, R, D = x.shape
    rm, rs = mod.shape[1], x_s.shape[1]
    tm = TM_IN
    n_in = w_in.shape[2]
    widths = (A_WIDTH, A_WIDTH, A_WIDTH, B_WIDTH, C_WIDTH, C_WIDTH)
    dtypes_p = (BF16, F32, F32, F32, BF16, BF16)
    dtypes_s = (BF16, F32, F32, F32, F32, F32)
    vec = lambda n: pl.BlockSpec((1, n), lambda g_, i: (0, 0))
    return pl.pallas_call(
        functools.partial(_in_proj_body, layer=layer),
        grid=(G, R // tm),
        in_specs=[
            pl.BlockSpec((1, tm, D), lambda g_, i: (g_, i, 0)),
            pl.BlockSpec((1, rm, D), lambda g_, i: (g_, 0, 0)),
            pl.BlockSpec((1, rm, D), lambda g_, i: (g_, 0, 1)),
            pl.BlockSpec((1, rs, D), lambda g_, i: (0, 0, 0)),
            pl.BlockSpec((1, rs, D), lambda g_, i: (0, 0, 0)),
            pl.BlockSpec((1, rs, D), lambda g_, i: (0, 0, 1)),
            vec(D), vec(A_WIDTH), vec(A_WIDTH), vec(C_WIDTH),
            pl.BlockSpec(memory_space=pl.ANY),
        ],
        out_specs=([pl.BlockSpec((1, tm, wd), lambda g_, i: (g_, i, 0)) for wd in widths]
                   + [pl.BlockSpec((1, rs, wd), lambda g_, i: (0, 0, 0)) for wd in widths]),
        out_shape=([jax.ShapeDtypeStruct((G, R, wd), dt) for wd, dt in zip(widths, dtypes_p)]
                   + [jax.ShapeDtypeStruct((1, rs, wd), dt) for wd, dt in zip(widths, dtypes_s)]),
        scratch_shapes=_resident_weight_scratch(D, n_in, W_CHUNK),
        compiler_params=_cparams(("arbitrary", "arbitrary")),
        name="in_proj",
    )(x, mod, mod, x_s, mod_s, mod_s, norm_g.reshape(1, D), q_gain.reshape(1, A_WIDTH), k_gain.reshape(1, A_WIDTH),
      sgu_gain.reshape(1, C_WIDTH), w_in)


def _attn_body(bias_ref, q_ref, k_ref, v_ref, u_ref, o_ref, acc_ref, run_ref, *, layer, blk):
    hp = pl.program_id(1)
    qi = pl.program_id(2)
    row = lax.broadcasted_iota(jnp.int32, (blk, blk), 0)
    col = lax.broadcasted_iota(jnp.int32, (blk, blk), 1)
    acc_ref[...] = jnp.zeros_like(acc_ref)
    run_ref[...] = jnp.zeros_like(run_ref)

    def kv_block(j, masked):
        ks = pl.multiple_of(j * blk, blk)
        for e0 in range(0, ATT_HEADS, ATT_GROUP):
            head_group(ks, range(e0, e0 + ATT_GROUP), masked)

    def head_group(ks, heads, masked):
        a_bf, lbs = {}, {}
        for e in heads:
            cols = slice(e * HEAD_DIM, (e + 1) * HEAD_DIM)
            bias2 = bias_ref[layer, hp * ATT_HEADS + e] * LOG2E
            kb = k_ref[0, pl.ds(ks, blk), cols].astype(BF16)
            z2 = lax.dot_general(q_ref[0, :, cols], kb, _NT, preferred_element_type=F32) + bias2
            a, lbs[e] = _stick_logs2(z2)
            if masked:
                a = jnp.where(col < row, a, 0.0)
            a_bf[e] = a.astype(BF16)
        rs = {e: jnp.dot(a_bf[e], u_ref[...], preferred_element_type=F32) for e in heads}
        for e in heads:
            cols = slice(e * HEAD_DIM, (e + 1) * HEAD_DIM)
            seen = run_ref[e]
            later = rs[e][:, :blk] + jnp.concatenate([seen] * (blk // LANES), axis=-1)
            w = jnp.exp2(lbs[e] - later)
            if masked:
                w = jnp.where(col < row, w, 0.0)
            run_ref[e] = seen + rs[e][:, blk:]
            vb = v_ref[0, pl.ds(ks, blk), cols].astype(BF16)
            acc_ref[e] += jnp.dot(w.astype(BF16), vb, preferred_element_type=F32)

    kv_block(qi, True)

    def body(it, carry):
        kv_block(qi - 1 - it, False)
        return carry

    lax.fori_loop(0, qi, body, 0)
    for e in range(ATT_HEADS):
        o_ref[0, :, e * HEAD_DIM:(e + 1) * HEAD_DIM] = acc_ref[e].astype(o_ref.dtype)


def _prompt_attention(q, k, v, sb_bias, layer):
    G, R, _ = q.shape
    blk = ATT_BLK
    wd = ATT_HEADS * HEAD_DIM
    return pl.pallas_call(
        functools.partial(_attn_body, layer=layer, blk=blk),
        grid=(G, A_HEADS // ATT_HEADS, R // blk),
        in_specs=[
            pl.BlockSpec(memory_space=pltpu.SMEM),
            pl.BlockSpec((1, blk, wd), lambda b, h, i: (b, i, h)),
            pl.BlockSpec((1, R, wd), lambda b, h, i: (b, 0, h)),
            pl.BlockSpec((1, R, wd), lambda b, h, i: (b, 0, h)),
            pl.BlockSpec((blk, blk + LANES), lambda b, h, i: (0, 0)),
        ],
        out_specs=pl.BlockSpec((1, blk, wd), lambda b, h, i: (b, i, h)),
        out_shape=jax.ShapeDtypeStruct((G, R, A_WIDTH), BF16),
        scratch_shapes=[pltpu.VMEM((ATT_HEADS, blk, HEAD_DIM), F32), pltpu.VMEM((ATT_HEADS, blk, LANES), F32)],
        compiler_params=_cparams(("arbitrary", "arbitrary", "arbitrary")),
        name="prompt_attn",
    )(sb_bias, q, k, v, _suffix_matrix(blk))


def _conv_body(cur_ref, halo_ref, cw_ref, cb_ref, lg_ref, lb_ref, o_ref, xp_ref, xs_ref, *, tt):
    i = pl.program_id(1)
    xp_ref[0:CONV_HALO, :] = jnp.where(i > 0, halo_ref[0], 0.0)
    xp_ref[CONV_HALO:, :] = cur_ref[0]
    span = tt + CONV_HALO - SUBLANES
    for r in range(1, SUBLANES):
        xs_ref[r - 1] = xp_ref[pl.ds(r, span), :]
    first = CONV_HALO - (CONV_WIDTH - 1)
    y = cb_ref[...]
    for w in range(CONV_WIDTH):
        r = (first + w) % SUBLANES
        base = first + w - r
        win = xp_ref[pl.ds(base, tt), :] if r == 0 else xs_ref[r - 1, pl.ds(base, tt), :]
        y = y + cw_ref[w:w + 1, :] * win
    mu = jnp.mean(y, axis=-1, keepdims=True)
    yc = y - mu
    yn = yc * lax.rsqrt(jnp.mean(yc * yc, axis=-1, keepdims=True) + EPS)
    o_ref[0] = _silu(yn * lg_ref[...] + lb_ref[...]).astype(o_ref.dtype)


def _prompt_conv(xg, conv_w, conv_b, ln_g, ln_b):
    G, R, C = xg.shape
    tt = CONV_TT
    hb = tt // CONV_HALO
    row = lambda a: a.reshape(1, C)
    return pl.pallas_call(
        functools.partial(_conv_body, tt=tt),
        grid=(G, R // tt),
        in_specs=[
            pl.BlockSpec((1, tt, C), lambda b, i: (b, i, 0)),
            pl.BlockSpec((1, CONV_HALO, C), lambda b, i: (b, jnp.maximum(i * hb - 1, 0), 0)),
            pl.BlockSpec((CONV_WIDTH, C), lambda b, i: (0, 0)),
            pl.BlockSpec((1, C), lambda b, i: (0, 0)),
            pl.BlockSpec((1, C), lambda b, i: (0, 0)),
            pl.BlockSpec((1, C), lambda b, i: (0, 0)),
        ],
        out_specs=pl.BlockSpec((1, tt, C), lambda b, i: (b, i, 0)),
        out_shape=jax.ShapeDtypeStruct((G, R, C), BF16),
        scratch_shapes=[pltpu.VMEM((tt + CONV_HALO, C), F32),
                        pltpu.VMEM((SUBLANES - 1, tt + CONV_HALO - SUBLANES, C), F32)],
        compiler_params=_cparams(("arbitrary", "arbitrary")),
        name="prompt_conv",
    )(xg, xg, conv_w, row(conv_b), row(ln_g), row(ln_b))


def _sgu_body(u_ref, v_ref, w_ref, b_ref, o_ref, *, tt):
    t = lax.broadcasted_iota(jnp.int32, (CHUNK, CHUNK), 0)
    s = lax.broadcasted_iota(jnp.int32, (CHUNK, CHUNK), 1)
    for hh in range(C_HEADS):
        wc = jnp.where(s <= t, w_ref[hh], 0.0).astype(BF16)
        cols = slice(hh * HEAD_DIM, (hh + 1) * HEAD_DIM)
        for c in range(tt // CHUNK):
            rows = slice(c * CHUNK, (c + 1) * CHUNK)
            mixed = jnp.dot(wc, v_ref[0, rows, cols].astype(BF16), preferred_element_type=F32) + b_ref[:, cols]
            o_ref[0, rows, cols] = (u_ref[0, rows, cols].astype(F32) * mixed).astype(o_ref.dtype)


def _prompt_sgu(u, vn, sgu_w, bias_exp):
    G, R, C = u.shape
    tt = SGU_TT
    return pl.pallas_call(
        functools.partial(_sgu_body, tt=tt),
        grid=(G, R // tt),
        in_specs=[
            pl.BlockSpec((1, tt, C), lambda b, i: (b, i, 0)),
            pl.BlockSpec((1, tt, C), lambda b, i: (b, i, 0)),
            pl.BlockSpec((C_HEADS, CHUNK, CHUNK), lambda b, i: (0, 0, 0)),
            pl.BlockSpec((CHUNK, C), lambda b, i: (0, 0)),
        ],
        out_specs=pl.BlockSpec((1, tt, C), lambda b, i: (b, i, 0)),
        out_shape=jax.ShapeDtypeStruct((G, R, C), BF16),
        compiler_params=_cparams(("arbitrary", "arbitrary")),
        name="prompt_sgu",
    )(u, vn, sgu_w, bias_exp)


def _out_proj_body(oa_ref, ob_ref, oc_ref, x_ref, gate_ref, shift_ref, scale_ref,
                   oas_ref, obs_ref, ocs_ref, xs_ref, gates_ref, shifts_ref, scales_ref, g_ref, w_hbm,
                   o_ref, h_ref, os_ref, hs_ref, wbf_ref, stage_ref, sem, *, layer):
    def project(oa, ob, oc, x, gate, shift, scale, out_ref, hout_ref):
        rows = x.shape[1]
        sub = min(rows, OUT_SUB_ROWS)
        for m in range(rows // sub):
            r = slice(m * sub, (m + 1) * sub)
            acc = jnp.dot(oa[0, r, :], wbf_ref[0:A_WIDTH, :], preferred_element_type=F32)
            acc = acc + jnp.dot(ob[0, r, :], wbf_ref[A_WIDTH:A_WIDTH + B_WIDTH, :], preferred_element_type=F32)
            acc = acc + jnp.dot(oc[0, r, :], wbf_ref[A_WIDTH + B_WIDTH:, :], preferred_element_type=F32)
            y = x[0, r, :] + gate[0] * acc
            out_ref[0, r, :] = y
            hout_ref[0, r, :] = _norm_rows(y, g_ref[...], shift[0], scale[0])

    @pl.when(_first_row_tile(0))
    def _():
        _load_weights_bf16(w_hbm, layer, wbf_ref, stage_ref, sem)
        project(oas_ref, obs_ref, ocs_ref, xs_ref, gates_ref, shifts_ref, scales_ref, os_ref, hs_ref)

    project(oa_ref, ob_ref, oc_ref, x_ref, gate_ref, shift_ref, scale_ref, o_ref, h_ref)


def _row_specs(tm, width):
    return pl.BlockSpec((1, tm, width), lambda j, g_, i: (g_, i, 0))


def _sample_specs(rs, width):
    return pl.BlockSpec((1, rs, width), lambda j, g_, i: (0, 0, 0))


def _out_proj(o_p, o_s, w_out, layer, x, x_s, mod, mod_s, norm_g):
    G, R, D = x.shape
    rm, rs = mod.shape[1], x_s.shape[1]
    tm = TM_OUT
    widths = (A_WIDTH, B_WIDTH, C_WIDTH)
    chunk = lambda rows, c, prompt: pl.BlockSpec((1, rows, D), (lambda g_, i: (g_, 0, c)) if prompt else
                                                 (lambda g_, i: (0, 0, c)))
    rows_p = lambda wd: pl.BlockSpec((1, tm, wd), lambda g_, i: (g_, i, 0))
    rows_s = lambda wd: pl.BlockSpec((1, rs, wd), lambda g_, i: (0, 0, 0))
    return pl.pallas_call(
        functools.partial(_out_proj_body, layer=layer),
        grid=(G, R // tm),
        in_specs=([rows_p(wd) for wd in widths] + [rows_p(D)] + [chunk(rm, c, True) for c in (2, 3, 4)]
                  + [rows_s(wd) for wd in widths] + [rows_s(D)] + [chunk(rs, c, False) for c in (2, 3, 4)]
                  + [pl.BlockSpec((1, D), lambda g_, i: (0, 0)), pl.BlockSpec(memory_space=pl.ANY)]),
        out_specs=[rows_p(D), rows_p(D), rows_s(D), rows_s(D)],
        out_shape=[jax.ShapeDtypeStruct((G, R, D), F32), jax.ShapeDtypeStruct((G, R, D), BF16),
                   jax.ShapeDtypeStruct((1, rs, D), F32), jax.ShapeDtypeStruct((1, rs, D), BF16)],
        scratch_shapes=_resident_weight_scratch(D, D, W_CHUNK),
        compiler_params=_cparams(("arbitrary", "arbitrary")),
        name="out_proj",
    )(*o_p, x, mod, mod, mod, *o_s, x_s, mod_s, mod_s, mod_s, norm_g.reshape(1, D), w_out)


def _gate_up_body(h_ref, hs_ref, wg_ref, wu_ref, o_ref, os_ref, wg_bf, wu_bf):
    def project(x, out_ref):
        a = jnp.dot(x[0], wg_bf[...], preferred_element_type=F32)
        b = jnp.dot(x[0], wu_bf[...], preferred_element_type=F32)
        out_ref[0] = (_silu(a) * b).astype(out_ref.dtype)

    @pl.when(_first_row_tile(1))
    def _():
        wg_bf[...] = wg_ref[...].astype(BF16)
        wu_bf[...] = wu_ref[...].astype(BF16)
        project(hs_ref, os_ref)

    project(h_ref, o_ref)


def _gate_up(h, h_s, w_gate, w_up, layer, tm):
    G, R, K = h.shape
    rs = h_s.shape[1]
    tn = TN
    wspec = pl.BlockSpec((None, K, tn), lambda j, g_, i: (layer, 0, j))
    return pl.pallas_call(
        _gate_up_body,
        grid=(D_FF // tn, G, R // tm),
        in_specs=[_row_specs(tm, K), _sample_specs(rs, K), wspec, wspec],
        out_specs=[pl.BlockSpec((1, tm, tn), lambda j, g_, i: (g_, i, j)),
                   pl.BlockSpec((1, rs, tn), lambda j, g_, i: (0, 0, j))],
        out_shape=[jax.ShapeDtypeStruct((G, R, D_FF), BF16), jax.ShapeDtypeStruct((1, rs, D_FF), BF16)],
        scratch_shapes=[pltpu.VMEM((K, tn), BF16), pltpu.VMEM((K, tn), BF16)],
        compiler_params=_cparams(("arbitrary", "arbitrary", "arbitrary")),
        name="ffn_gate_up",
    )(h, h_s, w_gate, w_up)


def _down_body(h_ref, x_ref, gate_ref, hs_ref, xs_ref, gates_ref, w_hbm, o_ref, os_ref, wbf_ref, stage_ref, sem,
               *, layer):
    def project(hid, x, gate, out_ref):
        out_ref[0] = x[0] + gate[0] * jnp.dot(hid[0], wbf_ref[...], preferred_element_type=F32)

    @pl.when(_first_row_tile(0))
    def _():
        _load_weights_bf16(w_hbm, layer, wbf_ref, stage_ref, sem)
        project(hs_ref, xs_ref, gates_ref, os_ref)

    project(h_ref, x_ref, gate_ref, o_ref)


def _down(hid, hid_s, w_down, layer, x, x_s, mod, mod_s, gate_chunk):
    G, R, D = x.shape
    K = hid.shape[2]
    rm, rs = mod.shape[1], x_s.shape[1]
    tm = TM_DOWN
    return pl.pallas_call(
        functools.partial(_down_body, layer=layer),
        grid=(G, R // tm),
        in_specs=[
            pl.BlockSpec((1, tm, K), lambda g_, i: (g_, i, 0)),
            pl.BlockSpec((1, tm, D), lambda g_, i: (g_, i, 0)),
            pl.BlockSpec((1, rm, D), lambda g_, i: (g_, 0, gate_chunk)),
            pl.BlockSpec((1, rs, K), lambda g_, i: (0, 0, 0)),
            pl.BlockSpec((1, rs, D), lambda g_, i: (0, 0, 0)),
            pl.BlockSpec((1, rs, D), lambda g_, i: (0, 0, gate_chunk)),
            pl.BlockSpec(memory_space=pl.ANY),
        ],
        out_specs=[pl.BlockSpec((1, tm, D), lambda g_, i: (g_, i, 0)),
                   pl.BlockSpec((1, rs, D), lambda g_, i: (0, 0, 0))],
        out_shape=[jax.ShapeDtypeStruct((G, R, D), F32), jax.ShapeDtypeStruct((1, rs, D), F32)],
        scratch_shapes=_resident_weight_scratch(K, D, W_CHUNK_DOWN),
        compiler_params=_cparams(("arbitrary", "arbitrary")),
        name="ffn_down",
    )(hid, x, mod, hid_s, x_s, mod_s, w_down)


def _paged_attn_body(pt_ref, q_ref, bias_ref, u_ref, *refs, n_steps):
    del pt_ref
    P = PAGES_PER_STEP
    k_refs = refs[:P]
    v_refs = refs[P:2 * P]
    o_ref = refs[2 * P]
    acc_ref, run_ref = refs[2 * P + 1:]
    t = pl.program_id(1)
    rows = PAGE_SIZE * A_HEADS
    n_blk = rows // LANES

    @pl.when(t == 0)
    def _():
        acc_ref[...] = jnp.zeros_like(acc_ref)
        run_ref[...] = jnp.zeros_like(run_ref)

    q = q_ref[0]
    z2 = jnp.concatenate(
        [lax.dot_general(q, k_refs[p][...].astype(BF16), _NT, preferred_element_type=F32) for p in range(P)],
        axis=0) + bias_ref[...] * LOG2E
    lane = lax.broadcasted_iota(jnp.int32, (P * A_HEADS, rows), 1)
    head = lax.broadcasted_iota(jnp.int32, (P * A_HEADS, rows), 0)
    own = (lane % A_HEADS) == (head % A_HEADS)
    a, lb = _stick_logs2(z2)
    a = jnp.where(own, a, 0.0)
    stacked = jnp.concatenate([a[:, c * LANES:(c + 1) * LANES] for c in range(n_blk)], axis=0)
    r = jnp.dot(stacked.astype(BF16), u_ref[...], preferred_element_type=F32)
    seen = run_ref[...]
    later = []
    for p in range(P):
        blocks = [None] * n_blk
        for c in reversed(range(n_blk)):
            r0 = (c * P + p) * A_HEADS
            blocks[c] = r[r0:r0 + A_HEADS, :LANES] + seen
            seen = seen + r[r0:r0 + A_HEADS, LANES:]
        later.append(jnp.concatenate(blocks, axis=-1))
    run_ref[...] = seen
    w = jnp.where(own, jnp.exp2(lb - jnp.concatenate(later, axis=0)), 0.0)
    acc = acc_ref[...]
    for p in range(P):
        wp = w[p * A_HEADS:(p + 1) * A_HEADS].astype(BF16)
        acc = acc + jnp.dot(wp, v_refs[p][...].astype(BF16), preferred_element_type=F32)
    acc_ref[...] = acc

    @pl.when(t == n_steps - 1)
    def _():
        o_ref[0] = acc


def _paged_attention(q, cache_k, cache_v, page_table, sb_bias_l, layer):
    B, n_pages = page_table.shape
    P = PAGES_PER_STEP
    n_steps = n_pages // P
    rows = PAGE_SIZE * A_HEADS

    def page_spec(p):
        return pl.BlockSpec((None, None, rows, HEAD_DIM),
                            lambda b, t, pt: (layer, pt[b, n_pages - 1 - (t * P + p)], 0, 0))

    grid_spec = pltpu.PrefetchScalarGridSpec(
        num_scalar_prefetch=1,
        grid=(B, n_steps),
        in_specs=[
            pl.BlockSpec((1, A_HEADS, HEAD_DIM), lambda b, t, pt: (b, 0, 0)),
            pl.BlockSpec((P * A_HEADS, 1), lambda b, t, pt: (0, 0)),
            pl.BlockSpec((LANES, 2 * LANES), lambda b, t, pt: (0, 0)),
        ] + [page_spec(p) for p in range(P)] * 2,
        out_specs=pl.BlockSpec((1, A_HEADS, HEAD_DIM), lambda b, t, pt: (b, 0, 0)),
        scratch_shapes=[pltpu.VMEM((A_HEADS, HEAD_DIM), F32), pltpu.VMEM((A_HEADS, LANES), F32)],
    )
    return pl.pallas_call(
        functools.partial(_paged_attn_body, n_steps=n_steps),
        grid_spec=grid_spec,
        out_shape=jax.ShapeDtypeStruct((B, A_HEADS, HEAD_DIM), F32),
        compiler_params=_cparams(("arbitrary", "arbitrary")),
        name="paged_attn",
    )(page_table, q, jnp.tile(sb_bias_l, P).reshape(P * A_HEADS, 1), _suffix_matrix(LANES),
      *([cache_k] * P), *([cache_v] * P))


def _sample_mix_body(xg_ref, st_ref, cw_ref, cb_ref, lg_ref, lb_ref, u_ref, vn_ref, w00_ref, b0_ref,
                     ob_ref, oc_ref):
    hist = CONV_WIDTH - 1
    xg = xg_ref[0]
    y = jnp.sum(st_ref[...] * cw_ref[0:hist, :][None], axis=1) + xg * cw_ref[hist:CONV_WIDTH, :] + cb_ref[...]
    mu = jnp.mean(y, axis=-1, keepdims=True)
    yc = y - mu
    yn = yc * lax.rsqrt(jnp.mean(yc * yc, axis=-1, keepdims=True) + EPS)
    ob_ref[0] = _silu(yn * lg_ref[...] + lb_ref[...]).astype(ob_ref.dtype)
    mixed = vn_ref[0].astype(BF16).astype(F32) * w00_ref[...].astype(BF16).astype(F32) + b0_ref[...]
    oc_ref[0] = (u_ref[0] * mixed).astype(oc_ref.dtype)


def _sample_mix(xg, state, conv_w, conv_b, ln_g, ln_b, u, vn, w00e, b0e):
    _, B, C = xg.shape
    row = lambda a: a.reshape(1, C)
    out = jax.ShapeDtypeStruct((1, B, C), BF16)
    return pl.pallas_call(
        _sample_mix_body,
        out_shape=(out, out),
        compiler_params=pltpu.CompilerParams(vmem_limit_bytes=VMEM_LIMIT),
        name="sample_mix",
    )(xg, state, conv_w, row(conv_b), row(ln_g), row(ln_b), u, vn, row(w00e), row(b0e))


def kernel(x_prompt, x_sample, cache_k, cache_v, state_conv, page_table, c_prompt, c_sample, norm1_g, norm2_g, w_ada, b_ada, w_in, q_norm_g, k_norm_g, sb_bias, conv_w, conv_b, conv_ln_g, conv_ln_b, sgu_norm_g, sgu_w, sgu_b, w_out, w_gate, w_up, w_down):
    b_p, seq, _ = x_prompt.shape
    b_s = x_sample.shape[0]
    n_pool = cache_k.shape[1]
    tm_p = TM_PROMPT

    n_c = b_p + b_s
    pad = (-n_c) % SUBLANES
    c_all = jnp.concatenate([c_prompt, c_sample, jnp.zeros((pad, D_MODEL), F32)], axis=0)
    mod = _ada_mod(c_all, w_ada, b_ada)
    mod_p = mod[:, :b_p].reshape(DEPTH, b_p, 1, N_MOD * D_MODEL)
    mod_s = mod[:, b_p:n_c].reshape(DEPTH, 1, b_s, N_MOD * D_MODEL)

    ck = cache_k.reshape(DEPTH, n_pool, PAGE_SIZE * A_HEADS, HEAD_DIM)
    cv = cache_v.reshape(DEPTH, n_pool, PAGE_SIZE * A_HEADS, HEAD_DIM)

    y_p = x_prompt
    y_s = x_sample.reshape(1, b_s, D_MODEL)
    kp_rows, vp_rows, ks_rows, vs_rows, conv_p, conv_s, sgu_s = [], [], [], [], [], [], []
    for l in range(DEPTH):
        q_gain = jnp.tile(q_norm_g[l], A_HEADS)
        k_gain = jnp.tile(k_norm_g[l], A_HEADS)
        sgu_bias = jnp.repeat(sgu_b[l].T, HEAD_DIM, axis=1)

        (q, k, v, xg, u, vn, q_s, k_s, v_s, xg_s, u_s, vn_s) = _in_proj(
            y_p, y_s, mod_p[l], mod_s[l], norm1_g[l], w_in, l, q_gain, k_gain, sgu_norm_g[l])
        o_a = _prompt_attention(q, k, v, sb_bias, l)
        o_b = _prompt_conv(xg, conv_w[l], conv_b[l], conv_ln_g[l], conv_ln_b[l])
        o_c = _prompt_sgu(u, vn, sgu_w[l], sgu_bias)
        o_as = _paged_attention(q_s.reshape(b_s, A_HEADS, HEAD_DIM), ck, cv, page_table, sb_bias[l], l)
        o_as = o_as.reshape(1, b_s, A_WIDTH).astype(BF16)
        w00e = jnp.repeat(sgu_w[l][:, 0, 0], HEAD_DIM)
        b0e = jnp.repeat(sgu_b[l][:, 0], HEAD_DIM)
        o_bs, o_cs = _sample_mix(xg_s, state_conv[l], conv_w[l], conv_b[l], conv_ln_g[l], conv_ln_b[l],
                                 u_s, vn_s, w00e, b0e)
        y_p, h2, y_s, h2_s = _out_proj((o_a, o_b, o_c), (o_as, o_bs, o_cs), w_out, l, y_p, y_s,
                                       mod_p[l], mod_s[l], norm2_g[l])

        hid, hid_s = _gate_up(h2, h2_s, w_gate, w_up, l, tm_p)
        y_p, y_s = _down(hid, hid_s, w_down, l, y_p, y_s, mod_p[l], mod_s[l], 5)

        kp_rows.append(k.reshape(b_p, seq, A_HEADS, HEAD_DIM))
        vp_rows.append(v.reshape(b_p, seq, A_HEADS, HEAD_DIM))
        conv_p.append(xg[:, seq - (CONV_WIDTH - 1):, :])
        ks_rows.append(k_s.reshape(b_s, 1, A_HEADS, HEAD_DIM))
        vs_rows.append(v_s.reshape(b_s, 1, A_HEADS, HEAD_DIM))
        conv_s.append(jnp.concatenate([state_conv[l][:, 1:, :], xg_s.reshape(b_s, 1, B_WIDTH)], axis=1))
        sgu_s.append(vn_s.reshape(b_s, 1, C_WIDTH))

    return (y_p, y_s.reshape(b_s, 1, D_MODEL), jnp.stack(kp_rows), jnp.stack(vp_rows),
            jnp.stack(ks_rows), jnp.stack(vs_rows), jnp.stack(conv_p), jnp.stack(conv_s), jnp.stack(sgu_s))
```

```python
import functools

import jax
import jax.numpy as jnp
from jax import lax
from jax.experimental import pallas as pl
from jax.experimental.pallas import tpu as pltpu

F32 = jnp.float32
BF16 = jnp.bfloat16

D_MODEL = 2048
DEPTH = 4
PAGE_SIZE = 128
HEAD_DIM = 128
A_WIDTH = D_MODEL // 2
A_HEADS = A_WIDTH // HEAD_DIM
B_WIDTH = D_MODEL // 4
C_WIDTH = D_MODEL - A_WIDTH - B_WIDTH
C_HEADS = C_WIDTH // HEAD_DIM
CONV_WIDTH = 31
CHUNK = 128
D_FF = ((8 * D_MODEL // 3 + 255) // 256) * 256
N_MOD = 6
EPS = 1e-6

COL_Q = 0
COL_K = A_WIDTH
COL_V = 2 * A_WIDTH
COL_GLU_A = 3 * A_WIDTH
COL_GLU_G = COL_GLU_A + B_WIDTH
COL_U = COL_GLU_G + B_WIDTH
COL_VC = COL_U + C_WIDTH

LANES = 128
SUBLANES = 8
VMEM_LIMIT = 56 * 1024 * 1024
TM_PROMPT = 1024
TM_DOWN = 256
W_CHUNK_DOWN = 256
TM_IN = 256
TM_OUT = 512
OUT_SUB_ROWS = 512
W_CHUNK = 512
TN = 512
ATT_BLK = 256
ATT_HEADS = 8
ATT_GROUP = 4
CONV_TT = 256
CONV_HALO = 32
SGU_TT = 512
PAGES_PER_STEP = 16

_NT = (((1,), (1,)), ((), ()))
LOG2E = 1.4426950408889634
Q_SCALE2 = HEAD_DIM ** -0.5 * LOG2E


def _cparams(sem):
    return pltpu.CompilerParams(dimension_semantics=sem, vmem_limit_bytes=VMEM_LIMIT)


def _stick_logs2(z2):
    a2 = jnp.maximum(z2, 0.0) + jnp.log(1.0 + jnp.exp2(-jnp.abs(z2))) * LOG2E
    return a2, z2 - a2


def _silu(x):
    return x * jax.nn.sigmoid(x)


def _suffix_matrix(size):
    j = jnp.arange(size)[:, None]
    n = jnp.arange(size + LANES)[None, :]
    return jnp.logical_or(n >= size, j > n).astype(BF16)


def _ada_body(c_ref, w_ref, b_ref, o_ref):
    s = _silu(c_ref[...]).astype(BF16)
    o_ref[0] = jnp.dot(s, w_ref[...].astype(BF16), preferred_element_type=F32) + b_ref[0]


def _ada_mod(c_all, w_ada, b_ada):
    rows = c_all.shape[0]
    tn = 1024
    n_mod = w_ada.shape[2]
    return pl.pallas_call(
        _ada_body,
        grid=(DEPTH, n_mod // tn),
        in_specs=[
            pl.BlockSpec((rows, D_MODEL), lambda l, j: (0, 0)),
            pl.BlockSpec((None, D_MODEL, tn), lambda l, j: (l, 0, j)),
            pl.BlockSpec((1, 1, tn), lambda l, j: (l, 0, j)),
        ],
        out_specs=pl.BlockSpec((1, rows, tn), lambda l, j: (l, 0, j)),
        out_shape=jax.ShapeDtypeStruct((DEPTH, rows, n_mod), F32),
        compiler_params=_cparams(("arbitrary", "arbitrary")),
        name="ada_mod",
    )(c_all, w_ada, b_ada.reshape(DEPTH, 1, n_mod))


def _first_row_tile(g_axis):
    return jnp.logical_and(pl.program_id(g_axis) == 0, pl.program_id(g_axis + 1) == 0)


def _norm_rows(x, g, shift, scale):
    y = x * lax.rsqrt(jnp.mean(x * x, axis=-1, keepdims=True) + EPS)
    return ((y * g) * (1.0 + scale) + shift).astype(BF16)


def _rms(a):
    return a * lax.rsqrt(jnp.mean(a * a, axis=-1, keepdims=True) + EPS)


def _load_weights_bf16(w_hbm, layer, wbf_ref, stage_ref, sem):
    chunk = stage_ref.shape[2]
    n_chunks = wbf_ref.shape[1] // chunk

    def copy(c):
        slot = c % 2
        return pltpu.make_async_copy(w_hbm.at[layer, :, pl.ds(c * chunk, chunk)], stage_ref.at[slot], sem.at[slot])

    copy(0).start()
    for c in range(n_chunks):
        if c + 1 < n_chunks:
            copy(c + 1).start()
        copy(c).wait()
        wbf_ref[:, c * chunk:(c + 1) * chunk] = stage_ref[c % 2].astype(BF16)


def _resident_weight_scratch(K, N, chunk):
    return [pltpu.VMEM((K, N), BF16), pltpu.VMEM((2, K, chunk), F32), pltpu.SemaphoreType.DMA((2,))]


def _in_proj_body(x_ref, shift_ref, scale_ref, xs_ref, shifts_ref, scales_ref, g_ref, qg_ref, kg_ref, sg_ref, w_hbm,
                  *refs, layer):
    outs_p, outs_s = refs[:6], refs[6:12]
    wbf_ref, stage_ref, sem = refs[12:]

    def project(x, shift, scale, outs):
        h = _norm_rows(x, g_ref[...], shift, scale)
        seg = lambda c0, wd: jnp.dot(h, wbf_ref[:, c0:c0 + wd], preferred_element_type=F32)
        q_o, k_o, v_o, xg_o, u_o, vn_o = outs
        for o_ref, c0, gain_ref, post in ((q_o, COL_Q, qg_ref, Q_SCALE2), (k_o, COL_K, kg_ref, None)):
            a = seg(c0, A_WIDTH)
            heads = [_rms(a[:, hh * HEAD_DIM:(hh + 1) * HEAD_DIM]) for hh in range(A_HEADS)]
            normed = jnp.concatenate(heads, axis=-1) * gain_ref[...]
            o_ref[0] = (normed if post is None else normed * post).astype(o_ref.dtype)
        v_o[0] = seg(COL_V, A_WIDTH).astype(v_o.dtype)
        xg_o[0] = (seg(COL_GLU_A, B_WIDTH) * jax.nn.sigmoid(seg(COL_GLU_G, B_WIDTH))).astype(xg_o.dtype)
        u_o[0] = seg(COL_U, C_WIDTH).astype(u_o.dtype)
        vn_o[0] = (_rms(seg(COL_VC, C_WIDTH)) * sg_ref[...]).astype(vn_o.dtype)

    @pl.when(_first_row_tile(0))
    def _():
        _load_weights_bf16(w_hbm, layer, wbf_ref, stage_ref, sem)
        project(xs_ref[0], shifts_ref[0], scales_ref[0], outs_s)

    project(x_ref[0], shift_ref[0], scale_ref[0], outs_p)


def _in_proj(x, x_s, mod, mod_s, norm_g, w_in, layer, q_gain, k_gain, sgu_gain):
    G, R, D = x.shape
    rm, rs = mod.shape[1], x_s.shape[1]
    tm = TM_IN
    n_in = w_in.shape[2]
    widths = (A_WIDTH, A_WIDTH, A_WIDTH, B_WIDTH, C_WIDTH, C_WIDTH)
    dtypes_p = (BF16, F32, F32, F32, BF16, BF16)
    dtypes_s = (BF16, F32, F32, F32, F32, F32)
    vec = lambda n: pl.BlockSpec((1, n), lambda g_, i: (0, 0))
    return pl.pallas_call(
        functools.partial(_in_proj_body, layer=layer),
        grid=(G, R // tm),
        in_specs=[
            pl.BlockSpec((1, tm, D), lambda g_, i: (g_, i, 0)),
            pl.BlockSpec((1, rm, D), lambda g_, i: (g_, 0, 0)),
            pl.BlockSpec((1, rm, D), lambda g_, i: (g_, 0, 1)),
            pl.BlockSpec((1, rs, D), lambda g_, i: (0, 0, 0)),
            pl.BlockSpec((1, rs, D), lambda g_, i: (0, 0, 0)),
            pl.BlockSpec((1, rs, D), lambda g_, i: (0, 0, 1)),
            vec(D), vec(A_WIDTH), vec(A_WIDTH), vec(C_WIDTH),
            pl.BlockSpec(memory_space=pl.ANY),
        ],
        out_specs=([pl.BlockSpec((1, tm, wd), lambda g_, i: (g_, i, 0)) for wd in widths]
                   + [pl.BlockSpec((1, rs, wd), lambda g_, i: (0, 0, 0)) for wd in widths]),
        out_shape=([jax.ShapeDtypeStruct((G, R, wd), dt) for wd, dt in zip(widths, dtypes_p)]
                   + [jax.ShapeDtypeStruct((1, rs, wd), dt) for wd, dt in zip(widths, dtypes_s)]),
        scratch_shapes=_resident_weight_scratch(D, n_in, W_CHUNK),
        compiler_params=_cparams(("arbitrary", "arbitrary")),
        name="in_proj",
    )(x, mod, mod, x_s, mod_s, mod_s, norm_g.reshape(1, D), q_gain.reshape(1, A_WIDTH), k_gain.reshape(1, A_WIDTH),
      sgu_gain.reshape(1, C_WIDTH), w_in)


def _attn_body(bias_ref, q_ref, k_ref, v_ref, u_ref, o_ref, acc_ref, run_ref, *, layer, blk):
    hp = pl.program_id(1)
    qi = pl.program_id(2)
    row = lax.broadcasted_iota(jnp.int32, (blk, blk), 0)
    col = lax.broadcasted_iota(jnp.int32, (blk, blk), 1)
    acc_ref[...] = jnp.zeros_like(acc_ref)
    run_ref[...] = jnp.zeros_like(run_ref)

    def kv_block(j, masked):
        ks = pl.multiple_of(j * blk, blk)
        for e0 in range(0, ATT_HEADS, ATT_GROUP):
            head_group(ks, range(e0, e0 + ATT_GROUP), masked)

    def head_group(ks, heads, masked):
        a_bf, lbs = {}, {}
        for e in heads:
            cols = slice(e * HEAD_DIM, (e + 1) * HEAD_DIM)
            bias2 = bias_ref[layer, hp * ATT_HEADS + e] * LOG2E
            kb = k_ref[0, pl.ds(ks, blk), cols].astype(BF16)
            z2 = lax.dot_general(q_ref[0, :, cols], kb, _NT, preferred_element_type=F32) + bias2
            a, lbs[e] = _stick_logs2(z2)
            if masked:
                a = jnp.where(col < row, a, 0.0)
            a_bf[e] = a.astype(BF16)
        rs = {e: jnp.dot(a_bf[e], u_ref[...], preferred_element_type=F32) for e in heads}
        for e in heads:
            cols = slice(e * HEAD_DIM, (e + 1) * HEAD_DIM)
            seen = run_ref[e]
            later = rs[e][:, :blk] + jnp.concatenate([seen] * (blk // LANES), axis=-1)
            w = jnp.exp2(lbs[e] - later)
            if masked:
                w = jnp.where(col < row, w, 0.0)
            run_ref[e] = seen + rs[e][:, blk:]
            vb = v_ref[0, pl.ds(ks, blk), cols].astype(BF16)
            acc_ref[e] += jnp.dot(w.astype(BF16), vb, preferred_element_type=F32)

    kv_block(qi, True)

    def body(it, carry):
        kv_block(qi - 1 - it, False)
        return carry

    lax.fori_loop(0, qi, body, 0)
    for e in range(ATT_HEADS):
        o_ref[0, :, e * HEAD_DIM:(e + 1) * HEAD_DIM] = acc_ref[e].astype(o_ref.dtype)


def _prompt_attention(q, k, v, sb_bias, layer):
    G, R, _ = q.shape
    blk = ATT_BLK
    wd = ATT_HEADS * HEAD_DIM
    return pl.pallas_call(
        functools.partial(_attn_body, layer=layer, blk=blk),
        grid=(G, A_HEADS // ATT_HEADS, R // blk),
        in_specs=[
            pl.BlockSpec(memory_space=pltpu.SMEM),
            pl.BlockSpec((1, blk, wd), lambda b, h, i: (b, i, h)),
            pl.BlockSpec((1, R, wd), lambda b, h, i: (b, 0, h)),
            pl.BlockSpec((1, R, wd), lambda b, h, i: (b, 0, h)),
            pl.BlockSpec((blk, blk + LANES), lambda b, h, i: (0, 0)),
        ],
        out_specs=pl.BlockSpec((1, blk, wd), lambda b, h, i: (b, i, h)),
        out_shape=jax.ShapeDtypeStruct((G, R, A_WIDTH), BF16),
        scratch_shapes=[pltpu.VMEM((ATT_HEADS, blk, HEAD_DIM), F32), pltpu.VMEM((ATT_HEADS, blk, LANES), F32)],
        compiler_params=_cparams(("arbitrary", "arbitrary", "arbitrary")),
        name="prompt_attn",
    )(sb_bias, q, k, v, _suffix_matrix(blk))


def _conv_body(cur_ref, halo_ref, cw_ref, cb_ref, lg_ref, lb_ref, o_ref, xp_ref, xs_ref, *, tt):
    i = pl.program_id(1)
    xp_ref[0:CONV_HALO, :] = jnp.where(i > 0, halo_ref[0], 0.0)
    xp_ref[CONV_HALO:, :] = cur_ref[0]
    span = tt + CONV_HALO - SUBLANES
    for r in range(1, SUBLANES):
        xs_ref[r - 1] = xp_ref[pl.ds(r, span), :]
    first = CONV_HALO - (CONV_WIDTH - 1)
    y = cb_ref[...]
    for w in range(CONV_WIDTH):
        r = (first + w) % SUBLANES
        base = first + w - r
        win = xp_ref[pl.ds(base, tt), :] if r == 0 else xs_ref[r - 1, pl.ds(base, tt), :]
        y = y + cw_ref[w:w + 1, :] * win
    mu = jnp.mean(y, axis=-1, keepdims=True)
    yc = y - mu
    yn = yc * lax.rsqrt(jnp.mean(yc * yc, axis=-1, keepdims=True) + EPS)
    o_ref[0] = _silu(yn * lg_ref[...] + lb_ref[...]).astype(o_ref.dtype)


def _prompt_conv(xg, conv_w, conv_b, ln_g, ln_b):
    G, R, C = xg.shape
    tt = CONV_TT
    hb = tt // CONV_HALO
    row = lambda a: a.reshape(1, C)
    return pl.pallas_call(
        functools.partial(_conv_body, tt=tt),
        grid=(G, R // tt),
        in_specs=[
            pl.BlockSpec((1, tt, C), lambda b, i: (b, i, 0)),
            pl.BlockSpec((1, CONV_HALO, C), lambda b, i: (b, jnp.maximum(i * hb - 1, 0), 0)),
            pl.BlockSpec((CONV_WIDTH, C), lambda b, i: (0, 0)),
            pl.BlockSpec((1, C), lambda b, i: (0, 0)),
            pl.BlockSpec((1, C), lambda b, i: (0, 0)),
            pl.BlockSpec((1, C), lambda b, i: (0, 0)),
        ],
        out_specs=pl.BlockSpec((1, tt, C), lambda b, i: (b, i, 0)),
        out_shape=jax.ShapeDtypeStruct((G, R, C), BF16),
        scratch_shapes=[pltpu.VMEM((tt + CONV_HALO, C), F32),
                        pltpu.VMEM((SUBLANES - 1, tt + CONV_HALO - SUBLANES, C), F32)],
        compiler_params=_cparams(("arbitrary", "arbitrary")),
        name="prompt_conv",
    )(xg, xg, conv_w, row(conv_b), row(ln_g), row(ln_b))


def _sgu_body(u_ref, v_ref, w_ref, b_ref, o_ref, *, tt):
    t = lax.broadcasted_iota(jnp.int32, (CHUNK, CHUNK), 0)
    s = lax.broadcasted_iota(jnp.int32, (CHUNK, CHUNK), 1)
    for hh in range(C_HEADS):
        wc = jnp.where(s <= t, w_ref[hh], 0.0).astype(BF16)
        cols = slice(hh * HEAD_DIM, (hh + 1) * HEAD_DIM)
        for c in range(tt // CHUNK):
            rows = slice(c * CHUNK, (c + 1) * CHUNK)
            mixed = jnp.dot(wc, v_ref[0, rows, cols].astype(BF16), preferred_element_type=F32) + b_ref[:, cols]
            o_ref[0, rows, cols] = (u_ref[0, rows, cols].astype(F32) * mixed).astype(o_ref.dtype)


def _prompt_sgu(u, vn, sgu_w, bias_exp):
    G, R, C = u.shape
    tt = SGU_TT
    return pl.pallas_call(
        functools.partial(_sgu_body, tt=tt),
        grid=(G, R // tt),
        in_specs=[
            pl.BlockSpec((1, tt, C), lambda b, i: (b, i, 0)),
            pl.BlockSpec((1, tt, C), lambda b, i: (b, i, 0)),
            pl.BlockSpec((C_HEADS, CHUNK, CHUNK), lambda b, i: (0, 0, 0)),
            pl.BlockSpec((CHUNK, C), lambda b, i: (0, 0)),
        ],
        out_specs=pl.BlockSpec((1, tt, C), lambda b, i: (b, i, 0)),
        out_shape=jax.ShapeDtypeStruct((G, R, C), BF16),
        compiler_params=_cparams(("arbitrary", "arbitrary")),
        name="prompt_sgu",
    )(u, vn, sgu_w, bias_exp)


def _out_proj_body(oa_ref, ob_ref, oc_ref, x_ref, gate_ref, shift_ref, scale_ref,
                   oas_ref, obs_ref, ocs_ref, xs_ref, gates_ref, shifts_ref, scales_ref, g_ref, w_hbm,
                   o_ref, h_ref, os_ref, hs_ref, wbf_ref, stage_ref, sem, *, layer):
    def project(oa, ob, oc, x, gate, shift, scale, out_ref, hout_ref):
        rows = x.shape[1]
        sub = min(rows, OUT_SUB_ROWS)
        for m in range(rows // sub):
            r = slice(m * sub, (m + 1) * sub)
            acc = jnp.dot(oa[0, r, :], wbf_ref[0:A_WIDTH, :], preferred_element_type=F32)
            acc = acc + jnp.dot(ob[0, r, :], wbf_ref[A_WIDTH:A_WIDTH + B_WIDTH, :], preferred_element_type=F32)
            acc = acc + jnp.dot(oc[0, r, :], wbf_ref[A_WIDTH + B_WIDTH:, :], preferred_element_type=F32)
            y = x[0, r, :] + gate[0] * acc
            out_ref[0, r, :] = y
            hout_ref[0, r, :] = _norm_rows(y, g_ref[...], shift[0], scale[0])

    @pl.when(_first_row_tile(0))
    def _():
        _load_weights_bf16(w_hbm, layer, wbf_ref, stage_ref, sem)
        project(oas_ref, obs_ref, ocs_ref, xs_ref, gates_ref, shifts_ref, scales_ref, os_ref, hs_ref)

    project(oa_ref, ob_ref, oc_ref, x_ref, gate_ref, shift_ref, scale_ref, o_ref, h_ref)


def _row_specs(tm, width):
    return pl.BlockSpec((1, tm, width), lambda j, g_, i: (g_, i, 0))


def _sample_specs(rs, width):
    return pl.BlockSpec((1, rs, width), lambda j, g_, i: (0, 0, 0))


def _out_proj(o_p, o_s, w_out, layer, x, x_s, mod, mod_s, norm_g):
    G, R, D = x.shape
    rm, rs = mod.shape[1], x_s.shape[1]
    tm = TM_OUT
    widths = (A_WIDTH, B_WIDTH, C_WIDTH)
    chunk = lambda rows, c, prompt: pl.BlockSpec((1, rows, D), (lambda g_, i: (g_, 0, c)) if prompt else
                                                 (lambda g_, i: (0, 0, c)))
    rows_p = lambda wd: pl.BlockSpec((1, tm, wd), lambda g_, i: (g_, i, 0))
    rows_s = lambda wd: pl.BlockSpec((1, rs, wd), lambda g_, i: (0, 0, 0))
    return pl.pallas_call(
        functools.partial(_out_proj_body, layer=layer),
        grid=(G, R // tm),
        in_specs=([rows_p(wd) for wd in widths] + [rows_p(D)] + [chunk(rm, c, True) for c in (2, 3, 4)]
                  + [rows_s(wd) for wd in widths] + [rows_s(D)] + [chunk(rs, c, False) for c in (2, 3, 4)]
                  + [pl.BlockSpec((1, D), lambda g_, i: (0, 0)), pl.BlockSpec(memory_space=pl.ANY)]),
        out_specs=[rows_p(D), rows_p(D), rows_s(D), rows_s(D)],
        out_shape=[jax.ShapeDtypeStruct((G, R, D), F32), jax.ShapeDtypeStruct((G, R, D), BF16),
                   jax.ShapeDtypeStruct((1, rs, D), F32), jax.ShapeDtypeStruct((1, rs, D), BF16)],
        scratch_shapes=_resident_weight_scratch(D, D, W_CHUNK),
        compiler_params=_cparams(("arbitrary", "arbitrary")),
        name="out_proj",
    )(*o_p, x, mod, mod, mod, *o_s, x_s, mod_s, mod_s, mod_s, norm_g.reshape(1, D), w_out)


def _gate_up_body(h_ref, hs_ref, wg_ref, wu_ref, o_ref, os_ref, wg_bf, wu_bf):
    def project(x, out_ref):
        a = jnp.dot(x[0], wg_bf[...], preferred_element_type=F32)
        b = jnp.dot(x[0], wu_bf[...], preferred_element_type=F32)
        out_ref[0] = (_silu(a) * b).astype(out_ref.dtype)

    @pl.when(_first_row_tile(1))
    def _():
        wg_bf[...] = wg_ref[...].astype(BF16)
        wu_bf[...] = wu_ref[...].astype(BF16)
        project(hs_ref, os_ref)

    project(h_ref, o_ref)


def _gate_up(h, h_s, w_gate, w_up, layer, tm):
    G, R, K = h.shape
    rs = h_s.shape[1]
    tn = TN
    wspec = pl.BlockSpec((None, K, tn), lambda j, g_, i: (layer, 0, j))
    return pl.pallas_call(
        _gate_up_body,
        grid=(D_FF // tn, G, R // tm),
        in_specs=[_row_specs(tm, K), _sample_specs(rs, K), wspec, wspec],
        out_specs=[pl.BlockSpec((1, tm, tn), lambda j, g_, i: (g_, i, j)),
                   pl.BlockSpec((1, rs, tn), lambda j, g_, i: (0, 0, j))],
        out_shape=[jax.ShapeDtypeStruct((G, R, D_FF), BF16), jax.ShapeDtypeStruct((1, rs, D_FF), BF16)],
        scratch_shapes=[pltpu.VMEM((K, tn), BF16), pltpu.VMEM((K, tn), BF16)],
        compiler_params=_cparams(("arbitrary", "arbitrary", "arbitrary")),
        name="ffn_gate_up",
    )(h, h_s, w_gate, w_up)


def _down_body(h_ref, x_ref, gate_ref, hs_ref, xs_ref, gates_ref, w_hbm, o_ref, os_ref, wbf_ref, stage_ref, sem,
               *, layer):
    def project(hid, x, gate, out_ref):
        out_ref[0] = x[0] + gate[0] * jnp.dot(hid[0], wbf_ref[...], preferred_element_type=F32)

    @pl.when(_first_row_tile(0))
    def _():
        _load_weights_bf16(w_hbm, layer, wbf_ref, stage_ref, sem)
        project(hs_ref, xs_ref, gates_ref, os_ref)

    project(h_ref, x_ref, gate_ref, o_ref)


def _down(hid, hid_s, w_down, layer, x, x_s, mod, mod_s, gate_chunk):
    G, R, D = x.shape
    K = hid.shape[2]
    rm, rs = mod.shape[1], x_s.shape[1]
    tm = TM_DOWN
    return pl.pallas_call(
        functools.partial(_down_body, layer=layer),
        grid=(G, R // tm),
        in_specs=[
            pl.BlockSpec((1, tm, K), lambda g_, i: (g_, i, 0)),
            pl.BlockSpec((1, tm, D), lambda g_, i: (g_, i, 0)),
            pl.BlockSpec((1, rm, D), lambda g_, i: (g_, 0, gate_chunk)),
            pl.BlockSpec((1, rs, K), lambda g_, i: (0, 0, 0)),
            pl.BlockSpec((1, rs, D), lambda g_, i: (0, 0, 0)),
            pl.BlockSpec((1, rs, D), lambda g_, i: (0, 0, gate_chunk)),
            pl.BlockSpec(memory_space=pl.ANY),
        ],
        out_specs=[pl.BlockSpec((1, tm, D), lambda g_, i: (g_, i, 0)),
                   pl.BlockSpec((1, rs, D), lambda g_, i: (0, 0, 0))],
        out_shape=[jax.ShapeDtypeStruct((G, R, D), F32), jax.ShapeDtypeStruct((1, rs, D), F32)],
        scratch_shapes=_resident_weight_scratch(K, D, W_CHUNK_DOWN),
        compiler_params=_cparams(("arbitrary", "arbitrary")),
        name="ffn_down",
    )(hid, x, mod, hid_s, x_s, mod_s, w_down)


def _paged_attn_body(pt_ref, q_ref, bias_ref, u_ref, *refs, n_steps):
    del pt_ref
    P = PAGES_PER_STEP
    k_refs = refs[:P]
    v_refs = refs[P:2 * P]
    o_ref = refs[2 * P]
    acc_ref, run_ref = refs[2 * P + 1:]
    t = pl.program_id(1)
    rows = PAGE_SIZE * A_HEADS
    n_blk = rows // LANES

    @pl.when(t == 0)
    def _():
        acc_ref[...] = jnp.zeros_like(acc_ref)
        run_ref[...] = jnp.zeros_like(run_ref)

    q = q_ref[0]
    z2 = jnp.concatenate(
        [lax.dot_general(q, k_refs[p][...].astype(BF16), _NT, preferred_element_type=F32) for p in range(P)],
        axis=0) + bias_ref[...] * LOG2E
    lane = lax.broadcasted_iota(jnp.int32, (P * A_HEADS, rows), 1)
    head = lax.broadcasted_iota(jnp.int32, (P * A_HEADS, rows), 0)
    own = (lane % A_HEADS) == (head % A_HEADS)
    a, lb = _stick_logs2(z2)
    a = jnp.where(own, a, 0.0)
    stacked = jnp.concatenate([a[:, c * LANES:(c + 1) * LANES] for c in range(n_blk)], axis=0)
    r = jnp.dot(stacked.astype(BF16), u_ref[...], preferred_element_type=F32)
    seen = run_ref[...]
    later = []
    for p in range(P):
        blocks = [None] * n_blk
        for c in reversed(range(n_blk)):
            r0 = (c * P + p) * A_HEADS
            blocks[c] = r[r0:r0 + A_HEADS, :LANES] + seen
            seen = seen + r[r0:r0 + A_HEADS, LANES:]
        later.append(jnp.concatenate(blocks, axis=-1))
    run_ref[...] = seen
    w = jnp.where(own, jnp.exp2(lb - jnp.concatenate(later, axis=0)), 0.0)
    acc = acc_ref[...]
    for p in range(P):
        wp = w[p * A_HEADS:(p + 1) * A_HEADS].astype(BF16)
        acc = acc + jnp.dot(wp, v_refs[p][...].astype(BF16), preferred_element_type=F32)
    acc_ref[...] = acc

    @pl.when(t == n_steps - 1)
    def _():
        o_ref[0] = acc


def _paged_attention(q, cache_k, cache_v, page_table, sb_bias_l, layer):
    B, n_pages = page_table.shape
    P = PAGES_PER_STEP
    n_steps = n_pages // P
    rows = PAGE_SIZE * A_HEADS

    def page_spec(p):
        return pl.BlockSpec((None, None, rows, HEAD_DIM),
                            lambda b, t, pt: (layer, pt[b, n_pages - 1 - (t * P + p)], 0, 0))

    grid_spec = pltpu.PrefetchScalarGridSpec(
        num_scalar_prefetch=1,
        grid=(B, n_steps),
        in_specs=[
            pl.BlockSpec((1, A_HEADS, HEAD_DIM), lambda b, t, pt: (b, 0, 0)),
            pl.BlockSpec((P * A_HEADS, 1), lambda b, t, pt: (0, 0)),
            pl.BlockSpec((LANES, 2 * LANES), lambda b, t, pt: (0, 0)),
        ] + [page_spec(p) for p in range(P)] * 2,
        out_specs=pl.BlockSpec((1, A_HEADS, HEAD_DIM), lambda b, t, pt: (b, 0, 0)),
        scratch_shapes=[pltpu.VMEM((A_HEADS, HEAD_DIM), F32), pltpu.VMEM((A_HEADS, LANES), F32)],
    )
    return pl.pallas_call(
        functools.partial(_paged_attn_body, n_steps=n_steps),
        grid_spec=grid_spec,
        out_shape=jax.ShapeDtypeStruct((B, A_HEADS, HEAD_DIM), F32),
        compiler_params=_cparams(("arbitrary", "arbitrary")),
        name="paged_attn",
    )(page_table, q, jnp.tile(sb_bias_l, P).reshape(P * A_HEADS, 1), _suffix_matrix(LANES),
      *([cache_k] * P), *([cache_v] * P))


def _sample_mix_body(xg_ref, st_ref, cw_ref, cb_ref, lg_ref, lb_ref, u_ref, vn_ref, w00_ref, b0_ref,
                     ob_ref, oc_ref):
    hist = CONV_WIDTH - 1
    xg = xg_ref[0]
    y = jnp.sum(st_ref[...] * cw_ref[0:hist, :][None], axis=1) + xg * cw_ref[hist:CONV_WIDTH, :] + cb_ref[...]
    mu = jnp.mean(y, axis=-1, keepdims=True)
    yc = y - mu
    yn = yc * lax.rsqrt(jnp.mean(yc * yc, axis=-1, keepdims=True) + EPS)
    ob_ref[0] = _silu(yn * lg_ref[...] + lb_ref[...]).astype(ob_ref.dtype)
    mixed = vn_ref[0].astype(BF16).astype(F32) * w00_ref[...].astype(BF16).astype(F32) + b0_ref[...]
    oc_ref[0] = (u_ref[0] * mixed).astype(oc_ref.dtype)


def _sample_mix(xg, state, conv_w, conv_b, ln_g, ln_b, u, vn, w00e, b0e):
    _, B, C = xg.shape
    row = lambda a: a.reshape(1, C)
    out = jax.ShapeDtypeStruct((1, B, C), BF16)
    return pl.pallas_call(
        _sample_mix_body,
        out_shape=(out, out),
        compiler_params=pltpu.CompilerParams(vmem_limit_bytes=VMEM_LIMIT),
        name="sample_mix",
    )(xg, state, conv_w, row(conv_b), row(ln_g), row(ln_b), u, vn, row(w00e), row(b0e))


def kernel(x_prompt, x_sample, cache_k, cache_v, state_conv, page_table, c_prompt, c_sample, norm1_g, norm2_g, w_ada, b_ada, w_in, q_norm_g, k_norm_g, sb_bias, conv_w, conv_b, conv_ln_g, conv_ln_b, sgu_norm_g, sgu_w, sgu_b, w_out, w_gate, w_up, w_down):
    b_p, seq, _ = x_prompt.shape
    b_s = x_sample.shape[0]
    n_pool = cache_k.shape[1]
    tm_p = TM_PROMPT

    n_c = b_p + b_s
    pad = (-n_c) % SUBLANES
    c_all = jnp.concatenate([c_prompt, c_sample, jnp.zeros((pad, D_MODEL), F32)], axis=0)
    mod = _ada_mod(c_all, w_ada, b_ada)
    mod_p = mod[:, :b_p].reshape(DEPTH, b_p, 1, N_MOD * D_MODEL)
    mod_s = mod[:, b_p:n_c].reshape(DEPTH, 1, b_s, N_MOD * D_MODEL)

    ck = cache_k.reshape(DEPTH, n_pool, PAGE_SIZE * A_HEADS, HEAD_DIM)
    cv = cache_v.reshape(DEPTH, n_pool, PAGE_SIZE * A_HEADS, HEAD_DIM)

    y_p = x_prompt
    y_s = x_sample.reshape(1, b_s, D_MODEL)
    kp_rows, vp_rows, ks_rows, vs_rows, conv_p, conv_s, sgu_s = [], [], [], [], [], [], []
    for l in range(DEPTH):
        q_gain = jnp.tile(q_norm_g[l], A_HEADS)
        k_gain = jnp.tile(k_norm_g[l], A_HEADS)
        sgu_bias = jnp.repeat(sgu_b[l].T, HEAD_DIM, axis=1)

        (q, k, v, xg, u, vn, q_s, k_s, v_s, xg_s, u_s, vn_s) = _in_proj(
            y_p, y_s, mod_p[l], mod_s[l], norm1_g[l], w_in, l, q_gain, k_gain, sgu_norm_g[l])
        o_a = _prompt_attention(q, k, v, sb_bias, l)
        o_b = _prompt_conv(xg, conv_w[l], conv_b[l], conv_ln_g[l], conv_ln_b[l])
        o_c = _prompt_sgu(u, vn, sgu_w[l], sgu_bias)
        o_as = _paged_attention(q_s.reshape(b_s, A_HEADS, HEAD_DIM), ck, cv, page_table, sb_bias[l], l)
        o_as = o_as.reshape(1, b_s, A_WIDTH).astype(BF16)
        w00e = jnp.repeat(sgu_w[l][:, 0, 0], HEAD_DIM)
        b0e = jnp.repeat(sgu_b[l][:, 0], HEAD_DIM)
        o_bs, o_cs = _sample_mix(xg_s, state_conv[l], conv_w[l], conv_b[l], conv_ln_g[l], conv_ln_b[l],
                                 u_s, vn_s, w00e, b0e)
        y_p, h2, y_s, h2_s = _out_proj((o_a, o_b, o_c), (o_as, o_bs, o_cs), w_out, l, y_p, y_s,
                                       mod_p[l], mod_s[l], norm2_g[l])

        hid, hid_s = _gate_up(h2, h2_s, w_gate, w_up, l, tm_p)
        y_p, y_s = _down(hid, hid_s, w_down, l, y_p, y_s, mod_p[l], mod_s[l], 5)

        kp_rows.append(k.reshape(b_p, seq, A_HEADS, HEAD_DIM))
        vp_rows.append(v.reshape(b_p, seq, A_HEADS, HEAD_DIM))
        conv_p.append(xg[:, seq - (CONV_WIDTH - 1):, :])
        ks_rows.append(k_s.reshape(b_s, 1, A_HEADS, HEAD_DIM))
        vs_rows.append(v_s.reshape(b_s, 1, A_HEADS, HEAD_DIM))
        conv_s.append(jnp.concatenate([state_conv[l][:, 1:, :], xg_s.reshape(b_s, 1, B_WIDTH)], axis=1))
        sgu_s.append(vn_s.reshape(b_s, 1, C_WIDTH))

    return (y_p, y_s.reshape(b_s, 1, D_MODEL), jnp.stack(kp_rows), jnp.stack(vp_rows),
            jnp.stack(ks_rows), jnp.stack(vs_rows), jnp.stack(conv_p), jnp.stack(conv_s), jnp.stack(sgu_s))
```
